```python
import jax, jax.numpy as jnp
from jax import lax
import numpy as np

D_MODEL = 2048
BATCH = 1
SEQ = 8192
DEPTH = 1

D_MIX = D_MODEL
D_LRU = D_MIX // 2
D_RET = D_MIX - D_LRU
LRU_BLOCKS = 8
LRU_BLOCK_W = D_LRU // LRU_BLOCKS
CONV_W = 4
LRU_C = 8.0
RET_HEADS = 8
RET_HD = D_RET // RET_HEADS
CHUNK = 128
ROPE_BASE = 10000.0
D_FF = ((8 * D_MODEL + 3 * 256 - 1) // (3 * 256)) * 256
EPS = 1e-6
SPLITS = (D_LRU, 2 * D_LRU, 2 * D_LRU + D_RET, 2 * D_LRU + 2 * D_RET, 2 * D_LRU + 3 * D_RET)
D_IN = 2 * D_LRU + 4 * D_RET

kernel_name = "hymba_rglru_retention_hybrid"


def rmsnorm(x, w):
    xf = x.astype(jnp.float32)
    y = xf * lax.rsqrt(jnp.mean(xf * xf, axis=-1, keepdims=True) + EPS)
    return (y * w.astype(jnp.float32)).astype(x.dtype)


def causal_depthwise_conv(x, w, b):
    S = x.shape[1]
    xp = jnp.pad(x, ((0, 0), (CONV_W - 1, 0), (0, 0)))
    y = b
    for tap in range(CONV_W):
        y = y + xp[:, tap:tap + S, :] * w[tap]
    return y


def rg_lru(u, wa, ba, wx, bx, lam):
    B, S, _ = u.shape
    uf = u.astype(jnp.float32)
    ub = uf.reshape(B, S, LRU_BLOCKS, LRU_BLOCK_W)
    r = jax.nn.sigmoid(jnp.einsum('bsnc,ncd->bsnd', ub, wa.astype(jnp.float32)).reshape(B, S, D_LRU) + ba)
    i = jax.nn.sigmoid(jnp.einsum('bsnc,ncd->bsnd', ub, wx.astype(jnp.float32)).reshape(B, S, D_LRU) + bx)
    log_a = LRU_C * r * jax.nn.log_sigmoid(lam.astype(jnp.float32))
    a = jnp.exp(log_a)
    b = jnp.sqrt(-jnp.expm1(2.0 * log_a)) * (i * uf)

    def combine(left, right):
        a1, b1 = left
        a2, b2 = right
        return a1 * a2, a2 * b1 + b2

    _, h = lax.associative_scan(combine, (a, b), axis=1)
    return h.astype(u.dtype)


def rope(t, cos, sin):
    half = t.shape[-1] // 2
    t1, t2 = t[..., :half], t[..., half:]
    c = cos[None, :, None, :]
    s = sin[None, :, None, :]
    return jnp.concatenate([t1 * c - t2 * s, t1 * s + t2 * c], axis=-1)


def retention(q, k, v, g, gn_w):
    B, S, _ = q.shape
    H, Dh, C = RET_HEADS, RET_HD, CHUNK
    N = S // C
    f32 = jnp.float32
    pos = jnp.arange(S, dtype=f32)
    inv_freq = ROPE_BASE ** (-jnp.arange(0, Dh, 2, dtype=f32) / Dh)
    ang = pos[:, None] * inv_freq[None, :]
    cos, sin = jnp.cos(ang), jnp.sin(ang)
    qh = rope(q.astype(f32).reshape(B, S, H, Dh), cos, sin)
    kh = rope(k.astype(f32).reshape(B, S, H, Dh), cos, sin) * (Dh ** -0.5)
    vh = v.astype(f32).reshape(B, S, H, Dh)
    qc = qh.reshape(B, N, C, H, Dh)
    kc = kh.reshape(B, N, C, H, Dh)
    vc = vh.reshape(B, N, C, H, Dh)

    log_gamma = jnp.log1p(-jnp.exp2(-5.0 - jnp.arange(H, dtype=f32)))
    idx = jnp.arange(C)
    diff = idx[:, None] - idx[None, :]
    causal = diff >= 0
    decay = jnp.where(causal[None], jnp.exp(log_gamma[:, None, None] * jnp.where(causal, diff, 0)[None].astype(f32)), 0.0)

    scores = jnp.einsum('bnqhd,bnkhd->bnhqk', qc, kc) * decay[None, None]
    inner = jnp.einsum('bnhqk,bnkhe->bnqhe', scores, vc)

    zeta = jnp.exp(log_gamma[None, :] * (C - 1 - idx).astype(f32)[:, None])
    kv = jnp.einsum('bnkhd,bnkhe->bnhde', kc * zeta[:, :, None], vc)
    chunk_decay = jnp.exp(log_gamma * C)[:, None, None]

    def step(R, kv_n):
        return R * chunk_decay + kv_n, R

    _, R_prev = lax.scan(step, jnp.zeros((B, H, Dh, Dh), f32), jnp.moveaxis(kv, 1, 0))
    R_prev = jnp.moveaxis(R_prev, 0, 1)

    xi = jnp.exp(log_gamma[None, :] * (idx + 1).astype(f32)[:, None])
    cross = jnp.einsum('bnqhd,bnhde->bnqhe', qc * xi[:, :, None], R_prev)
    o = (inner + cross).reshape(B, S, H, Dh)

    mu = jnp.mean(o, axis=-1, keepdims=True)
    var = jnp.mean(jnp.square(o - mu), axis=-1, keepdims=True)
    on = ((o - mu) * lax.rsqrt(var + EPS)).reshape(B, S, D_RET) * gn_w.astype(f32)
    return (jax.nn.silu(g.astype(f32)) * on).astype(q.dtype)


def setup_inputs(seed: int = 0) -> dict:
    key = jax.random.key(seed)
    ks = jax.random.split(key, 20)
    f32 = jnp.float32

    def nrm(k, shape, scale):
        return jax.random.normal(k, shape, f32) * scale

    u = jax.random.uniform(ks[8], (DEPTH, D_LRU), f32, 0.9, 0.999)
    s = u ** (1.0 / LRU_C)
    lru_lambda = jnp.log(s) - jnp.log1p(-s)
    return {
        "x": nrm(ks[0], (BATCH, SEQ, D_MODEL), 1.0),
        "ln1_w": 1.0 + nrm(ks[1], (DEPTH, D_MODEL), 0.02),
        "w_in": nrm(ks[2], (DEPTH, D_MODEL, D_IN), D_MODEL ** -0.5),
        "conv_w": nrm(ks[3], (DEPTH, CONV_W, D_LRU), CONV_W ** -0.5),
        "conv_b": nrm(ks[4], (DEPTH, D_LRU), 0.01),
        "gate_a_w": nrm(ks[5], (DEPTH, LRU_BLOCKS, LRU_BLOCK_W, LRU_BLOCK_W), LRU_BLOCK_W ** -0.5),
        "gate_a_b": nrm(ks[6], (DEPTH, D_LRU), 0.01),
        "gate_x_w": nrm(ks[7], (DEPTH, LRU_BLOCKS, LRU_BLOCK_W, LRU_BLOCK_W), LRU_BLOCK_W ** -0.5),
        "gate_x_b": nrm(ks[9], (DEPTH, D_LRU), 0.01),
        "lru_lambda": lru_lambda,
        "ret_gn_w": 1.0 + nrm(ks[10], (DEPTH, D_RET), 0.02),
        "w_out": nrm(ks[11], (DEPTH, D_MIX, D_MODEL), D_MIX ** -0.5),
        "ln2_w": 1.0 + nrm(ks[12], (DEPTH, D_MODEL), 0.02),
        "w_ffn_gate": nrm(ks[13], (DEPTH, D_MODEL, D_FF), D_MODEL ** -0.5),
        "w_ffn_up": nrm(ks[14], (DEPTH, D_MODEL, D_FF), D_MODEL ** -0.5),
        "w_ffn_down": nrm(ks[15], (DEPTH, D_FF, D_MODEL), D_FF ** -0.5),
        "final_norm_w": 1.0 + nrm(ks[16], (D_MODEL,), 0.02),
    }


def reference(x, ln1_w, w_in, conv_w, conv_b, gate_a_w, gate_a_b, gate_x_w, gate_x_b,
              lru_lambda, ret_gn_w, w_out, ln2_w, w_ffn_gate, w_ffn_up, w_ffn_down, final_norm_w):
    h = x
    for l in range(DEPTH):
        u = rmsnorm(h, ln1_w[l])
        proj = jnp.einsum('bsd,de->bse', u, w_in[l])
        lru_x, lru_g, q, k, v, ret_g = jnp.split(proj, SPLITS, axis=-1)
        lru_x = causal_depthwise_conv(lru_x, conv_w[l], conv_b[l])
        y_lru = rg_lru(lru_x, gate_a_w[l], gate_a_b[l], gate_x_w[l], gate_x_b[l], lru_lambda[l])
        y_lru = y_lru * jax.nn.gelu(lru_g)
        y_ret = retention(q, k, v, ret_g, ret_gn_w[l])
        y = jnp.concatenate([y_lru, y_ret], axis=-1)
        h = h + jnp.einsum('bse,ed->bsd', y, w_out[l])
        u = rmsnorm(h, ln2_w[l])
        ff = jax.nn.silu(jnp.einsum('bsd,df->bsf', u, w_ffn_gate[l])) * jnp.einsum('bsd,df->bsf', u, w_ffn_up[l])
        h = h + jnp.einsum('bsf,fd->bsd', ff, w_ffn_down[l])
    return rmsnorm(h, final_norm_w)
```

```python
import functools

import jax
import jax.numpy as jnp
from jax import lax
from jax.experimental import pallas as pl
from jax.experimental.pallas import tpu as pltpu

D_MODEL = 2048
SEQ = 8192
D_LRU = 1024
D_RET = 1024
LRU_BLOCKS = 8
LRU_BLOCK_W = 128
CONV_W = 4
LRU_C = 8.0
RET_HEADS = 8
RET_HD = 128
CHUNK = 128
ROPE_BASE = 10000.0
D_FF = 5632
D_IN = 6144
EPS = 1e-6

SUBLANES = 8
LANES = 128
VMEM_LIMIT_BYTES = 56 * 1024 * 1024

F32 = jnp.float32
BF16 = jnp.bfloat16

NORM_ROWS = 128
IN_TS, IN_TN = 1024, 1024
LRU_TB, LRU_R = 1024, 128
RET_TC = 512
OUT_TS = 512
FFN_TS, FFN_TF = 512, 512


def _rmsnorm_rows(x, w):
    ms = jnp.mean(x * x, axis=-1, keepdims=True)
    return (x * lax.rsqrt(ms + EPS)) * w


def _sigmoid(x):
    return 1.0 / (1.0 + jnp.exp(-x))


def _params(*sem):
    return pltpu.CompilerParams(dimension_semantics=sem, vmem_limit_bytes=VMEM_LIMIT_BYTES)


def _inproj_body(x_ref, lnw_ref, w_ref, o_ref, u_ref):
    @pl.when(pl.program_id(1) == 0)
    def _():
        def step(c, carry):
            r0 = pl.multiple_of(c * NORM_ROWS, NORM_ROWS)
            x = x_ref[pl.ds(r0, NORM_ROWS), :]
            u_ref[pl.ds(r0, NORM_ROWS), :] = _rmsnorm_rows(x, lnw_ref[...]).astype(BF16)
            return carry

        lax.fori_loop(0, IN_TS // NORM_ROWS, step, 0)

    o_ref[...] = jnp.dot(u_ref[...], w_ref[...], preferred_element_type=F32)


def _inproj(x, lnw, w):
    return pl.pallas_call(
        _inproj_body,
        grid=(SEQ // IN_TS, D_IN // IN_TN),
        in_specs=[
            pl.BlockSpec((IN_TS, D_MODEL), lambda i, j: (i, 0)),
            pl.BlockSpec((1, D_MODEL), lambda i, j: (0, 0)),
            pl.BlockSpec((D_MODEL, IN_TN), lambda i, j: (0, j)),
        ],
        out_specs=pl.BlockSpec((IN_TS, IN_TN), lambda i, j: (i, j)),
        out_shape=jax.ShapeDtypeStruct((SEQ, D_IN), F32),
        scratch_shapes=[pltpu.VMEM((IN_TS, D_MODEL), BF16)],
        compiler_params=_params("parallel", "arbitrary"),
        name="inproj",
    )(x, lnw, w)


def _shift_rows(x3, prev3, s, sub):
    return jnp.where(sub < s, pltpu.roll(prev3, s, 1), pltpu.roll(x3, s, 1))


def _lru_body(x_ref, g_ref, cw_ref, cb_ref, w_ref, ba_ref, bx_ref, lam_ref, o_ref, prev_ref, h_ref):
    @pl.when(pl.program_id(1) == 0)
    def _():
        prev_ref[...] = jnp.zeros_like(prev_ref)
        h_ref[...] = jnp.zeros_like(h_ref)

    nt = LRU_R // SUBLANES
    sub = lax.broadcasted_iota(jnp.int32, (nt, SUBLANES, LANES), 1)
    lam = lam_ref[...]
    lsl = -(jnp.maximum(-lam, 0.0) + jnp.log1p(jnp.exp(-jnp.abs(lam))))
    cw = cw_ref[...]
    cb = cb_ref[...]
    wcat = w_ref[...]
    ba = ba_ref[...]
    bx = bx_ref[...]

    def step(c, carry):
        prev8, hc = carry
        r0 = pl.multiple_of(c * LRU_R, LRU_R)
        x = x_ref[pl.ds(r0, LRU_R), :]
        x3 = x.reshape(nt, SUBLANES, LANES)
        prev3 = jnp.concatenate([prev8[None], x3[:-1]], axis=0)
        y3 = cb + _shift_rows(x3, prev3, 3, sub) * cw[0:1]
        y3 = y3 + _shift_rows(x3, prev3, 2, sub) * cw[1:2]
        y3 = y3 + _shift_rows(x3, prev3, 1, sub) * cw[2:3]
        y3 = y3 + x3 * cw[3:4]
        xc = y3.reshape(LRU_R, LANES)
        gates = jnp.dot(xc.astype(BF16), wcat, preferred_element_type=F32)
        r = _sigmoid(gates[:, :LANES] + ba)
        i = _sigmoid(gates[:, LANES:] + bx)
        log_a = (LRU_C * r) * lsl
        a = jnp.exp(log_a)
        th = jnp.tanh(log_a)
        b = jnp.sqrt(-(2.0 * th) / (1.0 - th)) * (i * xc)
        a3 = a.reshape(nt, SUBLANES, LANES)
        b3 = b.reshape(nt, SUBLANES, LANES)
        for s in (1, 2, 4):
            a_sh = jnp.where(sub >= s, pltpu.roll(a3, s, 1), 1.0)
            b_sh = jnp.where(sub >= s, pltpu.roll(b3, s, 1), 0.0)
            b3 = a3 * b_sh + b3
            a3 = a3 * a_sh
        tiles = []
        for j in range(nt):
            hj = a3[j] * hc + b3[j]
            hc = hj[SUBLANES - 1:SUBLANES, :]
            tiles.append(hj)
        h = jnp.concatenate(tiles, axis=0)
        g = g_ref[pl.ds(r0, LRU_R), :]
        gelu = 0.5 * g * (1.0 + jnp.tanh(0.7978845608028654 * (g + 0.044715 * (g * g * g))))
        o_ref[pl.ds(r0, LRU_R), :] = (h * gelu).astype(BF16)
        return x3[nt - 1], hc

    prev8, hc = lax.fori_loop(0, LRU_TB // LRU_R, step, (prev_ref[...], h_ref[0:1, :]))
    prev_ref[...] = prev8
    h_ref[...] = jnp.broadcast_to(hc, (SUBLANES, LANES))


def _lru(proj, conv_w, conv_b, wcat, ba, bx, lam):
    nb = D_LRU // LANES
    vec = lambda: pl.BlockSpec((1, LANES), lambda n, t: (0, n))
    return pl.pallas_call(
        _lru_body,
        grid=(nb, SEQ // LRU_TB),
        in_specs=[
            pl.BlockSpec((LRU_TB, LANES), lambda n, t: (t, n)),
            pl.BlockSpec((LRU_TB, LANES), lambda n, t: (t, nb + n)),
            pl.BlockSpec((CONV_W, LANES), lambda n, t: (0, n)),
            vec(),
            pl.BlockSpec((None, LANES, 2 * LANES), lambda n, t: (n, 0, 0)),
            vec(), vec(), vec(),
        ],
        out_specs=pl.BlockSpec((LRU_TB, LANES), lambda n, t: (t, n)),
        out_shape=jax.ShapeDtypeStruct((SEQ, D_LRU), BF16),
        scratch_shapes=[pltpu.VMEM((SUBLANES, LANES), F32), pltpu.VMEM((SUBLANES, LANES), F32)],
        compiler_params=_params("parallel", "arbitrary"),
        name="lru",
    )(proj, proj, conv_w, conv_b, wcat, ba, bx, lam)


def _ret_body(q_ref, k_ref, v_ref, g_ref, cq_ref, sq_ref, ck_ref, sk_ref, dec_ref, xi_ref, zeta_ref,
              gc_ref, gnw_ref, o_ref, state_ref):
    @pl.when(pl.program_id(0) == 0)
    def _():
        state_ref[...] = jnp.zeros_like(state_ref)

    def chunk(c, carry):
        r0 = pl.multiple_of(c * CHUNK, CHUNK)
        rows = pl.ds(r0, CHUNK)
        cq = cq_ref[rows, :]
        sq = sq_ref[rows, :]
        ck = ck_ref[rows, :]
        sk = sk_ref[rows, :]
        for h in range(RET_HEADS):
            cols = slice(h * RET_HD, (h + 1) * RET_HD)
            q = q_ref[rows, cols]
            k = k_ref[rows, cols]
            v = v_ref[rows, cols]
            qr = q * cq + pltpu.roll(q, RET_HD // 2, 1) * sq
            kr = k * ck + pltpu.roll(k, RET_HD // 2, 1) * sk
            qb = qr.astype(BF16)
            kb = kr.astype(BF16)
            vb = v.astype(BF16)
            scores = lax.dot_general(qb, kb, (((1,), (1,)), ((), ())), preferred_element_type=F32)
            scores = scores * dec_ref[h]
            inner = jnp.dot(scores.astype(BF16), vb, preferred_element_type=F32)
            state = state_ref[h]
            cross = jnp.dot((qr * xi_ref[:, cols]).astype(BF16), state.astype(BF16),
                            preferred_element_type=F32)
            o = inner + cross
            kz = (kr * zeta_ref[:, cols]).astype(BF16)
            kv = lax.dot_general(kz, vb, (((0,), (0,)), ((), ())), preferred_element_type=F32)
            state_ref[h] = state * gc_ref[h:h + 1, :] + kv
            mu = jnp.mean(o, axis=-1, keepdims=True)
            d = o - mu
            var = jnp.mean(d * d, axis=-1, keepdims=True)
            on = (d * lax.rsqrt(var + EPS)) * gnw_ref[:, cols]
            g = g_ref[rows, cols]
            o_ref[rows, cols] = ((g * _sigmoid(g)) * on).astype(BF16)
        return carry

    lax.fori_loop(0, RET_TC // CHUNK, chunk, 0)


def _retention(proj, cq, sq, ck, sk, decay, xi, zeta, gc, gnw):
    slab = lambda j: pl.BlockSpec((RET_TC, D_RET), lambda t, j=j: (t, j))
    tab = lambda: pl.BlockSpec((RET_TC, RET_HD), lambda t: (t, 0))
    full2 = lambda a: pl.BlockSpec(a.shape, lambda t: (0, 0))
    return pl.pallas_call(
        _ret_body,
        grid=(SEQ // RET_TC,),
        in_specs=[
            slab(2), slab(3), slab(4), slab(5),
            tab(), tab(), tab(), tab(),
            pl.BlockSpec(decay.shape, lambda t: (0, 0, 0)),
            full2(xi), full2(zeta), full2(gc), full2(gnw),
        ],
        out_specs=pl.BlockSpec((RET_TC, D_RET), lambda t: (t, 0)),
        out_shape=jax.ShapeDtypeStruct((SEQ, D_RET), BF16),
        scratch_shapes=[pltpu.VMEM((RET_HEADS, RET_HD, RET_HD), F32)],
        compiler_params=_params("arbitrary"),
        name="retention",
    )(proj, proj, proj, proj, cq, sq, ck, sk, decay, xi, zeta, gc, gnw)


def _outproj_body(x_ref, yl_ref, yr_ref, wo_ref, o_ref):
    acc = jnp.dot(yl_ref[...], wo_ref[0:D_LRU, :], preferred_element_type=F32)
    acc = acc + jnp.dot(yr_ref[...], wo_ref[D_LRU:D_LRU + D_RET, :], preferred_element_type=F32)
    o_ref[...] = x_ref[...] + acc


def _outproj(x, y_lru, y_ret, w_out):
    return pl.pallas_call(
        _outproj_body,
        grid=(SEQ // OUT_TS,),
        in_specs=[
            pl.BlockSpec((OUT_TS, D_MODEL), lambda i: (i, 0)),
            pl.BlockSpec((OUT_TS, D_LRU), lambda i: (i, 0)),
            pl.BlockSpec((OUT_TS, D_RET), lambda i: (i, 0)),
            pl.BlockSpec((D_MODEL, D_MODEL), lambda i: (0, 0)),
        ],
        out_specs=pl.BlockSpec((OUT_TS, D_MODEL), lambda i: (i, 0)),
        out_shape=jax.ShapeDtypeStruct((SEQ, D_MODEL), F32),
        compiler_params=_params("parallel"),
        name="outproj",
    )(x, y_lru, y_ret, w_out)


def _ffn_body(h_ref, ln2_ref, wg_ref, wu_ref, wd_ref, fnw_ref, o_ref, u_ref):
    f = pl.program_id(1)

    @pl.when(f == 0)
    def _():
        def step(c, carry):
            r0 = pl.multiple_of(c * NORM_ROWS, NORM_ROWS)
            hrows = h_ref[pl.ds(r0, NORM_ROWS), :]
            u_ref[pl.ds(r0, NORM_ROWS), :] = _rmsnorm_rows(hrows, ln2_ref[...]).astype(BF16)
            o_ref[pl.ds(r0, NORM_ROWS), :] = hrows
            return carry

        lax.fori_loop(0, FFN_TS // NORM_ROWS, step, 0)

    u = u_ref[...]
    gate = jnp.dot(u, wg_ref[...], preferred_element_type=F32)
    up = jnp.dot(u, wu_ref[...], preferred_element_type=F32)
    act = ((gate * _sigmoid(gate)) * up).astype(BF16)
    o_ref[...] += jnp.dot(act, wd_ref[...], preferred_element_type=F32)

    @pl.when(f == pl.num_programs(1) - 1)
    def _():
        def step(c, carry):
            r0 = pl.multiple_of(c * NORM_ROWS, NORM_ROWS)
            rows = pl.ds(r0, NORM_ROWS)
            o_ref[rows, :] = _rmsnorm_rows(o_ref[rows, :], fnw_ref[...])
            return carry

        lax.fori_loop(0, FFN_TS // NORM_ROWS, step, 0)


def _ffn(h1, ln2, wg, wu, wd, fnw):
    return pl.pallas_call(
        _ffn_body,
        grid=(SEQ // FFN_TS, D_FF // FFN_TF),
        in_specs=[
            pl.BlockSpec((FFN_TS, D_MODEL), lambda i, f: (i, 0)),
            pl.BlockSpec((1, D_MODEL), lambda i, f: (0, 0)),
            pl.BlockSpec((D_MODEL, FFN_TF), lambda i, f: (0, f)),
            pl.BlockSpec((D_MODEL, FFN_TF), lambda i, f: (0, f)),
            pl.BlockSpec((FFN_TF, D_MODEL), lambda i, f: (f, 0)),
            pl.BlockSpec((1, D_MODEL), lambda i, f: (0, 0)),
        ],
        out_specs=pl.BlockSpec((FFN_TS, D_MODEL), lambda i, f: (i, 0)),
        out_shape=jax.ShapeDtypeStruct((SEQ, D_MODEL), F32),
        scratch_shapes=[pltpu.VMEM((FFN_TS, D_MODEL), BF16)],
        compiler_params=_params("parallel", "arbitrary"),
        name="ffn",
    )(h1, ln2, wg, wu, wd, fnw)


def _retention_tables():
    H, Dh, C = RET_HEADS, RET_HD, CHUNK
    pos = jnp.arange(SEQ, dtype=F32)
    inv_freq = ROPE_BASE ** (-jnp.arange(0, Dh, 2, dtype=F32) / Dh)
    ang = pos[:, None] * inv_freq[None, :]
    cos, sin = jnp.cos(ang), jnp.sin(ang)
    cq = jnp.concatenate([cos, cos], axis=-1)
    sq = jnp.concatenate([-sin, sin], axis=-1)
    scale = Dh ** -0.5
    log_gamma = jnp.log1p(-jnp.exp2(-5.0 - jnp.arange(H, dtype=F32)))
    idx = jnp.arange(C)
    diff = idx[:, None] - idx[None, :]
    causal = diff >= 0
    decay = jnp.where(causal[None],
                      jnp.exp(log_gamma[:, None, None] * jnp.where(causal, diff, 0)[None].astype(F32)), 0.0)
    zeta = jnp.exp(log_gamma[None, :] * (C - 1 - idx).astype(F32)[:, None])
    xi = jnp.exp(log_gamma[None, :] * (idx + 1).astype(F32)[:, None])
    gc = jnp.exp(log_gamma * C)
    rep = lambda t: jnp.repeat(t, Dh, axis=1)
    return cq, sq, cq * scale, sq * scale, decay, rep(xi), rep(zeta), jnp.broadcast_to(gc[:, None], (H, Dh))


def kernel(x, ln1_w, w_in, conv_w, conv_b, gate_a_w, gate_a_b, gate_x_w, gate_x_b, lru_lambda, ret_gn_w,
           w_out, ln2_w, w_ffn_gate, w_ffn_up, w_ffn_down, final_norm_w):
    x2 = x.reshape(SEQ, D_MODEL)
    row = lambda v: v.reshape(1, -1)
    wcat = jnp.concatenate([gate_a_w[0], gate_x_w[0]], axis=-1).astype(BF16)
    proj = _inproj(x2, row(ln1_w[0]), w_in[0].astype(BF16))
    y_lru = _lru(proj, conv_w[0], row(conv_b[0]), wcat, row(gate_a_b[0]), row(gate_x_b[0]),
                 row(lru_lambda[0]))
    y_ret = _retention(proj, *_retention_tables(), row(ret_gn_w[0]))
    h1 = _outproj(x2, y_lru, y_ret, w_out[0].astype(BF16))
    out = _ffn(h1, row(ln2_w[0]), w_ffn_gate[0].astype(BF16), w_ffn_up[0].astype(BF16),
               w_ffn_down[0].astype(BF16), row(final_norm_w))
    return out.reshape(1, SEQ, D_MODEL)
```

```python
import functools

import jax
import jax.numpy as jnp
from jax import lax
from jax.experimental import pallas as pl
from jax.experimental.pallas import tpu as pltpu

D_MODEL = 2048
SEQ = 8192
D_LRU = 1024
D_RET = 1024
LRU_BLOCKS = 8
LRU_BLOCK_W = 128
CONV_W = 4
LRU_C = 8.0
RET_HEADS = 8
RET_HD = 128
CHUNK = 128
ROPE_BASE = 10000.0
D_FF = 5632
D_IN = 6144
EPS = 1e-6
GELU_C = 0.7978845608028654

SUBLANES = 8
LANES = 128
VMEM_LIMIT_BYTES = 56 * 1024 * 1024

F32 = jnp.float32
BF16 = jnp.bfloat16

NORM_ROWS = 128
IN_TS, IN_TN = 1024, 1024
LRU_TB, LRU_R, LRU_UNROLL = 1024, 128, 4
RET_TC = 512
RET_C = 256
OUT_TS = 512
FFN_TS, FFN_TF = 512, 512


def _rmsnorm_rows(x, w):
    ms = jnp.mean(x * x, axis=-1, keepdims=True)
    return (x * lax.rsqrt(ms + EPS)) * w


def _sigmoid(x):
    return 0.5 * jnp.tanh(0.5 * x) + 0.5


def _params(*sem):
    return pltpu.CompilerParams(dimension_semantics=sem, vmem_limit_bytes=VMEM_LIMIT_BYTES)


def _inproj_body(x_ref, lnw_ref, w_ref, o_ref, u_ref):
    @pl.when(pl.program_id(1) == 0)
    def _():
        def step(c, carry):
            r0 = pl.multiple_of(c * NORM_ROWS, NORM_ROWS)
            x = x_ref[pl.ds(r0, NORM_ROWS), :]
            u_ref[pl.ds(r0, NORM_ROWS), :] = _rmsnorm_rows(x, lnw_ref[...]).astype(BF16)
            return carry

        lax.fori_loop(0, IN_TS // NORM_ROWS, step, 0)

    o_ref[...] = jnp.dot(u_ref[...], w_ref[...], preferred_element_type=F32)


def _inproj(x, lnw, w):
    return pl.pallas_call(
        _inproj_body,
        grid=(SEQ // IN_TS, D_IN // IN_TN),
        in_specs=[
            pl.BlockSpec((IN_TS, D_MODEL), lambda i, j: (i, 0)),
            pl.BlockSpec((1, D_MODEL), lambda i, j: (0, 0)),
            pl.BlockSpec((D_MODEL, IN_TN), lambda i, j: (0, j)),
        ],
        out_specs=pl.BlockSpec((IN_TS, IN_TN), lambda i, j: (i, j)),
        out_shape=jax.ShapeDtypeStruct((SEQ, D_IN), F32),
        scratch_shapes=[pltpu.VMEM((IN_TS, D_MODEL), BF16)],
        compiler_params=_params("parallel", "arbitrary"),
        name="inproj",
    )(x, lnw, w)


def _lru_body(x_ref, g_ref, cw_ref, cb_ref, w_ref, ba_ref, bx_ref, lam_ref, o_ref,
              xpad_ref, sa_ref, sb_ref, h_ref):
    nt = LRU_R // SUBLANES

    @pl.when(pl.program_id(1) == 0)
    def _():
        xpad_ref[0:SUBLANES, :] = jnp.zeros((SUBLANES, LANES), F32)
        h_ref[...] = jnp.zeros_like(h_ref)

    sa_ref[:, :, 0:SUBLANES, :] = jnp.ones((LRU_UNROLL, nt, SUBLANES, LANES), F32)
    sb_ref[:, :, 0:SUBLANES, :] = jnp.zeros((LRU_UNROLL, nt, SUBLANES, LANES), F32)
    xpad_ref[pl.ds(SUBLANES, LRU_TB), :] = x_ref[...]

    lam = lam_ref[...]
    lsl4 = -4.0 * (jnp.maximum(-lam, 0.0) + jnp.log1p(jnp.exp(-jnp.abs(lam))))
    cw = cw_ref[...]
    cb = cb_ref[...]
    wcat = w_ref[...]
    ba = ba_ref[...]
    bx = bx_ref[...]

    def sub_block(r0, hc, slot):
        base = r0 + SUBLANES
        xc = cb + xpad_ref[pl.ds(base - 3, LRU_R), :] * cw[0:1]
        xc = xc + xpad_ref[pl.ds(base - 2, LRU_R), :] * cw[1:2]
        xc = xc + xpad_ref[pl.ds(base - 1, LRU_R), :] * cw[2:3]
        xc = xc + xpad_ref[pl.ds(base, LRU_R), :] * cw[3:4]
        gates = jnp.dot(xc.astype(BF16), wcat, preferred_element_type=F32)
        tr = jnp.tanh(0.5 * (gates[:, :LANES] + ba))
        ti = jnp.tanh(0.5 * (gates[:, LANES:] + bx))
        log_a = (tr + 1.0) * lsl4
        a = jnp.exp(log_a)
        v = -jnp.tanh(log_a)
        coef = jnp.where(v > 0.0, v * lax.rsqrt((v + v) * (1.0 + v)), 0.0)
        b = coef * ((ti + 1.0) * xc)
        a3 = a.reshape(nt, SUBLANES, LANES)
        b3 = b.reshape(nt, SUBLANES, LANES)
        for s in (1, 2, 4):
            sa_ref[slot, :, SUBLANES:2 * SUBLANES, :] = a3
            sb_ref[slot, :, SUBLANES:2 * SUBLANES, :] = b3
            a_sh = sa_ref[slot, :, SUBLANES - s:2 * SUBLANES - s, :]
            b_sh = sb_ref[slot, :, SUBLANES - s:2 * SUBLANES - s, :]
            b3 = a3 * b_sh + b3
            a3 = a3 * a_sh
        tiles = []
        for j in range(nt):
            hj = a3[j] * hc + b3[j]
            hc = hj[SUBLANES - 1:SUBLANES, :]
            tiles.append(hj)
        h = jnp.concatenate(tiles, axis=0)
        g = g_ref[pl.ds(r0, LRU_R), :]
        tg = jnp.tanh(g * (GELU_C + (GELU_C * 0.044715) * (g * g)))
        hg = 0.5 * g
        o_ref[pl.ds(r0, LRU_R), :] = (h * (hg * tg + hg)).astype(BF16)
        return hc

    def step(c, hc):
        for slot in range(LRU_UNROLL):
            r0 = pl.multiple_of((c * LRU_UNROLL + slot) * LRU_R, LRU_R)
            hc = sub_block(r0, hc, slot)
        return hc

    hc = lax.fori_loop(0, LRU_TB // (LRU_R * LRU_UNROLL), step, h_ref[0:1, :])
    xpad_ref[0:SUBLANES, :] = xpad_ref[pl.ds(LRU_TB, SUBLANES), :]
    h_ref[...] = jnp.broadcast_to(hc, (SUBLANES, LANES))


def _lru(proj, conv_w, conv_b, wcat, ba, bx, lam):
    nb = D_LRU // LANES
    vec = lambda: pl.BlockSpec((1, LANES), lambda n, t: (0, n))
    return pl.pallas_call(
        _lru_body,
        grid=(nb, SEQ // LRU_TB),
        in_specs=[
            pl.BlockSpec((LRU_TB, LANES), lambda n, t: (t, n)),
            pl.BlockSpec((LRU_TB, LANES), lambda n, t: (t, nb + n)),
            pl.BlockSpec((CONV_W, LANES), lambda n, t: (0, n)),
            vec(),
            pl.BlockSpec((None, LANES, 2 * LANES), lambda n, t: (n, 0, 0)),
            vec(), vec(), vec(),
        ],
        out_specs=pl.BlockSpec((LRU_TB, LANES), lambda n, t: (t, n)),
        out_shape=jax.ShapeDtypeStruct((SEQ, D_LRU), BF16),
        scratch_shapes=[
            pltpu.VMEM((SUBLANES + LRU_TB, LANES), F32),
            pltpu.VMEM((LRU_UNROLL, LRU_R // SUBLANES, 2 * SUBLANES, LANES), F32),
            pltpu.VMEM((LRU_UNROLL, LRU_R // SUBLANES, 2 * SUBLANES, LANES), F32),
            pltpu.VMEM((SUBLANES, LANES), F32),
        ],
        compiler_params=_params("parallel", "arbitrary"),
        name="lru",
    )(proj, proj, conv_w, conv_b, wcat, ba, bx, lam)


def _ret_body(q_ref, k_ref, v_ref, g_ref, cq_ref, sq_ref, ck_ref, sk_ref, dec_ref, xi_ref, zeta_ref,
              gc_ref, gnw_ref, o_ref, state_ref):
    @pl.when(pl.program_id(0) == 0)
    def _():
        state_ref[...] = jnp.zeros_like(state_ref)

    mean_mat = jnp.full((RET_HD, RET_HD), 1.0 / RET_HD, BF16)
    for c in range(RET_TC // RET_C):
        rows = slice(c * RET_C, (c + 1) * RET_C)
        cq = cq_ref[rows, :]
        sq = sq_ref[rows, :]
        ck = ck_ref[rows, :]
        sk = sk_ref[rows, :]
        for h in range(RET_HEADS):
            cols = slice(h * RET_HD, (h + 1) * RET_HD)
            q = q_ref[rows, cols]
            k = k_ref[rows, cols]
            v = v_ref[rows, cols]
            qr = q * cq + pltpu.roll(q, RET_HD // 2, 1) * sq
            kr = k * ck + pltpu.roll(k, RET_HD // 2, 1) * sk
            qb = qr.astype(BF16)
            kb = kr.astype(BF16)
            vb = v.astype(BF16)
            scores = lax.dot_general(qb, kb, (((1,), (1,)), ((), ())), preferred_element_type=F32)
            p = (scores * dec_ref[h]).astype(BF16)
            state = state_ref[h]
            lhs = jnp.concatenate([p, (qr * xi_ref[:, cols]).astype(BF16)], axis=1)
            rhs = jnp.concatenate([vb, state.astype(BF16)], axis=0)
            o = jnp.dot(lhs, rhs, preferred_element_type=F32)
            kz = (kr * zeta_ref[:, cols]).astype(BF16)
            kv = lax.dot_general(kz, vb, (((0,), (0,)), ((), ())), preferred_element_type=F32)
            state_ref[h] = state * gc_ref[h:h + 1, :] + kv
            mu = jnp.dot(o.astype(BF16), mean_mat, preferred_element_type=F32)
            d = o - mu
            var = jnp.dot((d * d).astype(BF16), mean_mat, preferred_element_type=F32)
            on = (d * lax.rsqrt(var + EPS)) * gnw_ref[:, cols]
            g = g_ref[rows, cols]
            o_ref[rows, cols] = ((g * _sigmoid(g)) * on).astype(BF16)


def _retention(proj, cq, sq, ck, sk, decay, xi, zeta, gc, gnw):
    slab = lambda j: pl.BlockSpec((RET_TC, D_RET), lambda t, j=j: (t, j))
    tab = lambda: pl.BlockSpec((RET_TC, RET_HD), lambda t: (t, 0))
    full2 = lambda a: pl.BlockSpec(a.shape, lambda t: (0, 0))
    return pl.pallas_call(
        _ret_body,
        grid=(SEQ // RET_TC,),
        in_specs=[
            slab(2), slab(3), slab(4), slab(5),
            tab(), tab(), tab(), tab(),
            pl.BlockSpec(decay.shape, lambda t: (0, 0, 0)),
            full2(xi), full2(zeta), full2(gc), full2(gnw),
        ],
        out_specs=pl.BlockSpec((RET_TC, D_RET), lambda t: (t, 0)),
        out_shape=jax.ShapeDtypeStruct((SEQ, D_RET), BF16),
        scratch_shapes=[pltpu.VMEM((RET_HEADS, RET_HD, RET_HD), F32)],
        compiler_params=_params("arbitrary"),
        name="retention",
    )(proj, proj, proj, proj, cq, sq, ck, sk, decay, xi, zeta, gc, gnw)


def _outproj_body(x_ref, yl_ref, yr_ref, wo_ref, o_ref):
    acc = jnp.dot(yl_ref[...], wo_ref[0:D_LRU, :], preferred_element_type=F32)
    acc = acc + jnp.dot(yr_ref[...], wo_ref[D_LRU:D_LRU + D_RET, :], preferred_element_type=F32)
    o_ref[...] = x_ref[...] + acc


def _outproj(x, y_lru, y_ret, w_out):
    return pl.pallas_call(
        _outproj_body,
        grid=(SEQ // OUT_TS,),
        in_specs=[
            pl.BlockSpec((OUT_TS, D_MODEL), lambda i: (i, 0)),
            pl.BlockSpec((OUT_TS, D_LRU), lambda i: (i, 0)),
            pl.BlockSpec((OUT_TS, D_RET), lambda i: (i, 0)),
            pl.BlockSpec((D_MODEL, D_MODEL), lambda i: (0, 0)),
        ],
        out_specs=pl.BlockSpec((OUT_TS, D_MODEL), lambda i: (i, 0)),
        out_shape=jax.ShapeDtypeStruct((SEQ, D_MODEL), F32),
        compiler_params=_params("parallel"),
        name="outproj",
    )(x, y_lru, y_ret, w_out)


def _ffn_body(h_ref, ln2_ref, wg_ref, wu_ref, wd_ref, fnw_ref, o_ref, u_ref):
    f = pl.program_id(1)

    @pl.when(f == 0)
    def _():
        def step(c, carry):
            r0 = pl.multiple_of(c * NORM_ROWS, NORM_ROWS)
            hrows = h_ref[pl.ds(r0, NORM_ROWS), :]
            u_ref[pl.ds(r0, NORM_ROWS), :] = _rmsnorm_rows(hrows, ln2_ref[...]).astype(BF16)
            o_ref[pl.ds(r0, NORM_ROWS), :] = hrows
            return carry

        lax.fori_loop(0, FFN_TS // NORM_ROWS, step, 0)

    u = u_ref[...]
    gate = jnp.dot(u, wg_ref[...], preferred_element_type=F32)
    up = jnp.dot(u, wu_ref[...], preferred_element_type=F32)
    act = ((gate * _sigmoid(gate)) * up).astype(BF16)
    o_ref[...] += jnp.dot(act, wd_ref[...], preferred_element_type=F32)

    @pl.when(f == pl.num_programs(1) - 1)
    def _():
        def step(c, carry):
            r0 = pl.multiple_of(c * NORM_ROWS, NORM_ROWS)
            rows = pl.ds(r0, NORM_ROWS)
            o_ref[rows, :] = _rmsnorm_rows(o_ref[rows, :], fnw_ref[...])
            return carry

        lax.fori_loop(0, FFN_TS // NORM_ROWS, step, 0)


def _ffn(h1, ln2, wg, wu, wd, fnw):
    return pl.pallas_call(
        _ffn_body,
        grid=(SEQ // FFN_TS, D_FF // FFN_TF),
        in_specs=[
            pl.BlockSpec((FFN_TS, D_MODEL), lambda i, f: (i, 0)),
            pl.BlockSpec((1, D_MODEL), lambda i, f: (0, 0)),
            pl.BlockSpec((D_MODEL, FFN_TF), lambda i, f: (0, f)),
            pl.BlockSpec((D_MODEL, FFN_TF), lambda i, f: (0, f)),
            pl.BlockSpec((FFN_TF, D_MODEL), lambda i, f: (f, 0)),
            pl.BlockSpec((1, D_MODEL), lambda i, f: (0, 0)),
        ],
        out_specs=pl.BlockSpec((FFN_TS, D_MODEL), lambda i, f: (i, 0)),
        out_shape=jax.ShapeDtypeStruct((SEQ, D_MODEL), F32),
        scratch_shapes=[pltpu.VMEM((FFN_TS, D_MODEL), BF16)],
        compiler_params=_params("parallel", "arbitrary"),
        name="ffn",
    )(h1, ln2, wg, wu, wd, fnw)


def _retention_tables():
    H, Dh, C = RET_HEADS, RET_HD, RET_C
    pos = jnp.arange(SEQ, dtype=F32)
    inv_freq = ROPE_BASE ** (-jnp.arange(0, Dh, 2, dtype=F32) / Dh)
    ang = pos[:, None] * inv_freq[None, :]
    cos, sin = jnp.cos(ang), jnp.sin(ang)
    cq = jnp.concatenate([cos, cos], axis=-1)
    sq = jnp.concatenate([-sin, sin], axis=-1)
    scale = Dh ** -0.5
    log_gamma = jnp.log1p(-jnp.exp2(-5.0 - jnp.arange(H, dtype=F32)))
    idx = jnp.arange(C)
    diff = idx[:, None] - idx[None, :]
    causal = diff >= 0
    decay = jnp.where(causal[None],
                      jnp.exp(log_gamma[:, None, None] * jnp.where(causal, diff, 0)[None].astype(F32)), 0.0)
    zeta = jnp.exp(log_gamma[None, :] * (C - 1 - idx).astype(F32)[:, None])
    xi = jnp.exp(log_gamma[None, :] * (idx + 1).astype(F32)[:, None])
    gc = jnp.exp(log_gamma * C)
    rep = lambda t: jnp.repeat(t, Dh, axis=1)
    return cq, sq, cq * scale, sq * scale, decay, rep(xi), rep(zeta), jnp.broadcast_to(gc[:, None], (H, Dh))


def kernel(x, ln1_w, w_in, conv_w, conv_b, gate_a_w, gate_a_b, gate_x_w, gate_x_b, lru_lambda, ret_gn_w,
           w_out, ln2_w, w_ffn_gate, w_ffn_up, w_ffn_down, final_norm_w):
    x2 = x.reshape(SEQ, D_MODEL)
    row = lambda v: v.reshape(1, -1)
    wcat = jnp.concatenate([gate_a_w[0], gate_x_w[0]], axis=-1).astype(BF16)
    proj = _inproj(x2, row(ln1_w[0]), w_in[0].astype(BF16))
    y_lru = _lru(proj, conv_w[0], row(conv_b[0]), wcat, row(gate_a_b[0]), row(gate_x_b[0]),
                 row(lru_lambda[0]))
    y_ret = _retention(proj, *_retention_tables(), row(ret_gn_w[0]))
    h1 = _outproj(x2, y_lru, y_ret, w_out[0].astype(BF16))
    out = _ffn(h1, row(ln2_w[0]), w_ffn_gate[0].astype(BF16), w_ffn_up[0].astype(BF16),
               w_ffn_down[0].astype(BF16), row(final_norm_w))
    return out.reshape(1, SEQ, D_MODEL)
```

```python
import jax
import jax.numpy as jnp
import numpy as np
from jax import lax
from jax.experimental import pallas as pl
from jax.experimental.pallas import tpu as pltpu

D_MODEL = 2048
SEQ = 8192
D_LRU = 1024
D_RET = 1024
CONV_W = 4
RET_HEADS = 8
RET_HD = 128
ROPE_BASE = 10000.0
D_FF = 5632
D_IN = 6144
EPS = 1e-6
GELU_C = 0.7978845608028654

SUBLANES = 8
LANES = 128
VMEM_LIMIT_BYTES = 56 * 1024 * 1024

F32 = jnp.float32
BF16 = jnp.bfloat16

NORM_ROWS = 128
MIX_TS = 512
MIX_J = 8
MIX_NW = D_IN // MIX_J
MIX_LB = MIX_NW // LANES
N_BLK = SEQ // MIX_TS
N_LRU_LB = D_LRU // LANES
N_REST_LB = (D_IN - D_LRU) // LANES
LRU_R = 128
LRU_SUB = MIX_TS // LRU_R
RET_C = 256
OUT_TS = 512
FFN_TS, FFN_TF = 512, 512


def _rmsnorm_rows(x, w):
    ms = jnp.mean(x * x, axis=-1, keepdims=True)
    return (x * lax.rsqrt(ms + EPS)) * w


def _sigmoid(x):
    return 0.5 * jnp.tanh(0.5 * x) + 0.5


def _params(*sem):
    return pltpu.CompilerParams(dimension_semantics=sem, vmem_limit_bytes=VMEM_LIMIT_BYTES)


def _lru_sub_block(r0, hc, slot, m, prv, plru_ref, prest_ref, ysc_ref, sa_ref, sb_ref,
                   cw, cb, wcat, ba, bx, lsl4):
    nt = LRU_R // SUBLANES
    base = r0 + SUBLANES
    xc = cb + plru_ref[prv, m, pl.ds(base - 3, LRU_R), :] * cw[0:1]
    xc = xc + plru_ref[prv, m, pl.ds(base - 2, LRU_R), :] * cw[1:2]
    xc = xc + plru_ref[prv, m, pl.ds(base - 1, LRU_R), :] * cw[2:3]
    xc = xc + plru_ref[prv, m, pl.ds(base, LRU_R), :] * cw[3:4]
    gates = jnp.dot(xc.astype(BF16), wcat, preferred_element_type=F32)
    tr = jnp.tanh(0.5 * (gates[:, :LANES] + ba))
    ti = jnp.tanh(0.5 * (gates[:, LANES:] + bx))
    log_a = (tr + 1.0) * lsl4
    a = jnp.exp(log_a)
    v = -jnp.tanh(log_a)
    coef = jnp.where(v > 0.0, v * lax.rsqrt((v + v) * (1.0 + v)), 0.0)
    b = coef * ((ti + 1.0) * xc)
    a3 = a.reshape(nt, SUBLANES, LANES)
    b3 = b.reshape(nt, SUBLANES, LANES)
    for s in (1, 2, 4):
        sa_ref[slot, :, SUBLANES:2 * SUBLANES, :] = a3
        sb_ref[slot, :, SUBLANES:2 * SUBLANES, :] = b3
        a_sh = sa_ref[slot, :, SUBLANES - s:2 * SUBLANES - s, :]
        b_sh = sb_ref[slot, :, SUBLANES - s:2 * SUBLANES - s, :]
        b3 = a3 * b_sh + b3
        a3 = a3 * a_sh
    tiles = []
    for t in range(nt):
        ht = a3[t] * hc + b3[t]
        hc = ht[SUBLANES - 1:SUBLANES, :]
        tiles.append(ht)
    h = jnp.concatenate(tiles, axis=0)
    g = prest_ref[prv, m, pl.ds(r0, LRU_R), :].astype(F32)
    tg = jnp.tanh(g * (GELU_C + (GELU_C * 0.044715) * (g * g)))
    hg = 0.5 * g
    ysc_ref[m, pl.ds(r0, LRU_R), :] = (h * (hg * tg + hg)).astype(BF16)
    return hc


def _mix_body(x_ref, ln_ref, w_ref,
              cw_ref, cb_ref, wg_ref, ba_ref, bx_ref, lam_ref,
              cr_ref, sr_ref, cbase_ref, sbase_ref, dec_ref, xi_ref, zeta_ref, gc_ref, gnw_ref,
              fg_ref, fu_ref, fd_ref, fo_ref,
              y_ref, bg_ref, bu_ref, bd_ref, bo_ref,
              u_ref, plru_ref, prest_ref, ysc_ref, sa_ref, sb_ref, h_ref, state_ref,
              cq_ref, sq_ref, ck_ref, sk_ref):
    i = pl.program_id(0)
    j = pl.program_id(1)
    cur = lax.rem(i, 2)
    prv = 1 - cur
    m = j
    nt = LRU_R // SUBLANES

    bg_ref[...] = fg_ref[...].astype(BF16)
    bu_ref[...] = fu_ref[...].astype(BF16)
    bd_ref[...] = fd_ref[...].astype(BF16)
    bo_ref[...] = fo_ref[...].astype(BF16)

    @pl.when(jnp.logical_and(i == 0, j == 0))
    def _():
        plru_ref[...] = jnp.zeros_like(plru_ref)
        prest_ref[...] = jnp.zeros_like(prest_ref)

    @pl.when(jnp.logical_and(i == 1, j == 0))
    def _():
        h_ref[...] = jnp.zeros_like(h_ref)
        state_ref[...] = jnp.zeros_like(state_ref)
        plru_ref[0, 0:N_LRU_LB, 0:SUBLANES, :] = jnp.zeros((N_LRU_LB, SUBLANES, LANES), F32)

    @pl.when(j == 0)
    def _():
        def step(r, carry):
            r0 = pl.multiple_of(r * NORM_ROWS, NORM_ROWS)
            x = x_ref[pl.ds(r0, NORM_ROWS), :]
            u_ref[pl.ds(r0, NORM_ROWS), :] = _rmsnorm_rows(x, ln_ref[...]).astype(BF16)
            return carry

        lax.fori_loop(0, MIX_TS // NORM_ROWS, step, 0)

        blk = jnp.maximum(i - 1, 0)
        cb_, sb_ = cbase_ref[blk], sbase_ref[blk]
        cos = cb_ * cr_ref[...] - sb_ * sr_ref[...]
        sin = sb_ * cr_ref[...] + cb_ * sr_ref[...]
        lane = lax.broadcasted_iota(jnp.int32, (1, RET_HD), 1)
        sin = jnp.where(lane < RET_HD // 2, -sin, sin)
        cq_ref[...] = cos
        sq_ref[...] = sin
        ck_ref[...] = cos * (RET_HD ** -0.5)
        sk_ref[...] = sin * (RET_HD ** -0.5)

    sa_ref[:, :, 0:SUBLANES, :] = jnp.ones((LRU_SUB, nt, SUBLANES, LANES), F32)
    sb_ref[:, :, 0:SUBLANES, :] = jnp.zeros((LRU_SUB, nt, SUBLANES, LANES), F32)
    lam = lam_ref[m]
    lsl4 = -4.0 * (jnp.maximum(-lam, 0.0) + jnp.log1p(jnp.exp(-jnp.abs(lam))))
    cw = cw_ref[m]
    cb = cb_ref[m]
    wcat = wg_ref[m]
    ba = ba_ref[m]
    bx = bx_ref[m]

    def lru(slot, hc):
        return _lru_sub_block(slot * LRU_R, hc, slot, m, prv, plru_ref, prest_ref, ysc_ref, sa_ref, sb_ref,
                              cw, cb, wcat, ba, bx, lsl4)

    def ret_front(c):
        rows = slice(c * RET_C, (c + 1) * RET_C)
        q = prest_ref[prv, 8 + m, rows, :].astype(F32)
        k = prest_ref[prv, 16 + m, rows, :].astype(F32)
        qr = q * cq_ref[rows, :] + pltpu.roll(q, RET_HD // 2, 1) * sq_ref[rows, :]
        kr = k * ck_ref[rows, :] + pltpu.roll(k, RET_HD // 2, 1) * sk_ref[rows, :]
        scores = lax.dot_general(qr.astype(BF16), kr.astype(BF16), (((1,), (1,)), ((), ())),
                                 preferred_element_type=F32)
        p = (scores * dec_ref[m]).astype(BF16)
        return p, (qr * xi_ref[m]).astype(BF16), (kr * zeta_ref[m]).astype(BF16)

    def ret_back(c, state, front):
        p, qxi, kz = front
        rows = slice(c * RET_C, (c + 1) * RET_C)
        vb = prest_ref[prv, 24 + m, rows, :]
        g = prest_ref[prv, 32 + m, rows, :].astype(F32)
        lhs = jnp.concatenate([p, qxi], axis=1)
        rhs = jnp.concatenate([vb, state.astype(BF16)], axis=0)
        o = jnp.dot(lhs, rhs, preferred_element_type=F32)
        kv = lax.dot_general(kz, vb, (((0,), (0,)), ((), ())), preferred_element_type=F32)
        state = state * gc_ref[m] + kv
        mu = jnp.mean(o, axis=-1, keepdims=True)
        d = o - mu
        var = jnp.mean(d * d, axis=-1, keepdims=True)
        on = (d * lax.rsqrt(var + EPS)) * gnw_ref[m]
        ysc_ref[N_LRU_LB + m, rows, :] = ((g * _sigmoid(g)) * on).astype(BF16)
        return state

    hc = h_ref[m][0:1, :]
    state = state_ref[m]
    for slot in range(LRU_SUB):
        hc = lru(slot, hc)
    for c in range(MIX_TS // RET_C):
        state = ret_back(c, state, ret_front(c))

    res = jnp.dot(u_ref[...], w_ref[...], preferred_element_type=F32)
    for kk in range(MIX_LB):
        lb = j * MIX_LB + kk
        blk = res[:, kk * LANES:(kk + 1) * LANES]
        plru_ref[cur, jnp.minimum(lb, N_LRU_LB), pl.ds(SUBLANES, MIX_TS), :] = blk
        prest_ref[cur, jnp.where(lb >= N_LRU_LB, lb - N_LRU_LB, N_REST_LB), :, :] = blk.astype(BF16)

    h_ref[m] = jnp.broadcast_to(hc, (SUBLANES, LANES))
    state_ref[m] = state
    plru_ref[cur, m, 0:SUBLANES, :] = plru_ref[prv, m, pl.ds(MIX_TS, SUBLANES), :]

    @pl.when(j == MIX_J - 1)
    def _():
        for lb in range(2 * N_LRU_LB):
            y_ref[:, lb * LANES:(lb + 1) * LANES] = ysc_ref[lb]


def _cast_spec(rows, cols, n_chunks):
    return pl.BlockSpec((rows, cols), lambda i, j: (jnp.minimum(i * MIX_J + j, n_chunks - 1), 0))


def _mix(x, ln, w_in, cw, cb, wcat, ba, bx, lam, cr, sr, cbase, sbase, decay, xi, zeta, gc, gnw,
         w_gate, w_up, w_down, w_out):
    prev_blk = lambda i, j: (jnp.maximum(i - 1, 0), 0)
    const2 = lambda a: pl.BlockSpec(a.shape, lambda i, j: (0, 0))
    const3 = lambda a: pl.BlockSpec(a.shape, lambda i, j: (0, 0, 0))
    nt = LRU_R // SUBLANES
    n_steps = N_BLK * MIX_J
    up_rows = D_MODEL // n_steps
    down_rows = 4 * up_rows
    cast_specs = [
        _cast_spec(up_rows, D_FF, D_MODEL // up_rows),
        _cast_spec(up_rows, D_FF, D_MODEL // up_rows),
        _cast_spec(down_rows, D_MODEL, D_FF // down_rows),
        _cast_spec(up_rows, D_MODEL, D_MODEL // up_rows),
    ]
    return pl.pallas_call(
        _mix_body,
        grid=(N_BLK + 1, MIX_J),
        in_specs=[
            pl.BlockSpec((MIX_TS, D_MODEL), lambda i, j: (jnp.minimum(i, N_BLK - 1), 0)),
            pl.BlockSpec((1, D_MODEL), lambda i, j: (0, 0)),
            pl.BlockSpec((D_MODEL, MIX_NW), lambda i, j: (0, j)),
            const3(cw), const3(cb), const3(wcat), const3(ba), const3(bx), const3(lam),
            const2(cr), const2(sr), const3(cbase), const3(sbase),
            const3(decay), const3(xi), const3(zeta), const3(gc), const3(gnw),
            *cast_specs,
        ],
        out_specs=[pl.BlockSpec((MIX_TS, D_LRU + D_RET), prev_blk), *cast_specs],
        out_shape=[
            jax.ShapeDtypeStruct((SEQ, D_LRU + D_RET), BF16),
            jax.ShapeDtypeStruct(w_gate.shape, BF16),
            jax.ShapeDtypeStruct(w_up.shape, BF16),
            jax.ShapeDtypeStruct(w_down.shape, BF16),
            jax.ShapeDtypeStruct(w_out.shape, BF16),
        ],
        scratch_shapes=[
            pltpu.VMEM((MIX_TS, D_MODEL), BF16),
            pltpu.VMEM((2, N_LRU_LB + 1, SUBLANES + MIX_TS, LANES), F32),
            pltpu.VMEM((2, N_REST_LB + 1, MIX_TS, LANES), BF16),
            pltpu.VMEM((2 * N_LRU_LB, MIX_TS, LANES), BF16),
            pltpu.VMEM((LRU_SUB, nt, 2 * SUBLANES, LANES), F32),
            pltpu.VMEM((LRU_SUB, nt, 2 * SUBLANES, LANES), F32),
            pltpu.VMEM((N_LRU_LB, SUBLANES, LANES), F32),
            pltpu.VMEM((RET_HEADS, RET_HD, RET_HD), F32),
            pltpu.VMEM((MIX_TS, RET_HD), F32),
            pltpu.VMEM((MIX_TS, RET_HD), F32),
            pltpu.VMEM((MIX_TS, RET_HD), F32),
            pltpu.VMEM((MIX_TS, RET_HD), F32),
        ],
        compiler_params=_params("arbitrary", "arbitrary"),
        name="mix",
    )(x, ln, w_in, cw, cb, wcat, ba, bx, lam, cr, sr, cbase, sbase, decay, xi, zeta, gc, gnw,
      w_gate, w_up, w_down, w_out)


def _outproj_body(x_ref, y_ref, wo_ref, o_ref):
    o_ref[...] = x_ref[...] + jnp.dot(y_ref[...], wo_ref[...], preferred_element_type=F32)


def _outproj(x, y, w_out):
    return pl.pallas_call(
        _outproj_body,
        grid=(SEQ // OUT_TS,),
        in_specs=[
            pl.BlockSpec((OUT_TS, D_MODEL), lambda i: (i, 0)),
            pl.BlockSpec((OUT_TS, D_LRU + D_RET), lambda i: (i, 0)),
            pl.BlockSpec((D_MODEL, D_MODEL), lambda i: (0, 0)),
        ],
        out_specs=pl.BlockSpec((OUT_TS, D_MODEL), lambda i: (i, 0)),
        out_shape=jax.ShapeDtypeStruct((SEQ, D_MODEL), F32),
        compiler_params=_params("parallel"),
        name="outproj",
    )(x, y, w_out)


def _ffn_body(h_ref, ln2_ref, wg_ref, wu_ref, wd_ref, fnw_ref, o_ref, u_ref):
    f = pl.program_id(1)

    @pl.when(f == 0)
    def _():
        def step(c, carry):
            r0 = pl.multiple_of(c * NORM_ROWS, NORM_ROWS)
            hrows = h_ref[pl.ds(r0, NORM_ROWS), :]
            u_ref[pl.ds(r0, NORM_ROWS), :] = _rmsnorm_rows(hrows, ln2_ref[...]).astype(BF16)
            o_ref[pl.ds(r0, NORM_ROWS), :] = hrows
            return carry

        lax.fori_loop(0, FFN_TS // NORM_ROWS, step, 0)

    u = u_ref[...]
    gate = jnp.dot(u, wg_ref[...], preferred_element_type=F32)
    up = jnp.dot(u, wu_ref[...], preferred_element_type=F32)
    act = ((gate * _sigmoid(gate)) * up).astype(BF16)
    o_ref[...] += jnp.dot(act, wd_ref[...], preferred_element_type=F32)

    @pl.when(f == pl.num_programs(1) - 1)
    def _():
        def step(c, carry):
            r0 = pl.multiple_of(c * NORM_ROWS, NORM_ROWS)
            rows = pl.ds(r0, NORM_ROWS)
            o_ref[rows, :] = _rmsnorm_rows(o_ref[rows, :], fnw_ref[...])
            return carry

        lax.fori_loop(0, FFN_TS // NORM_ROWS, step, 0)


def _ffn(h1, ln2, wg, wu, wd, fnw):
    return pl.pallas_call(
        _ffn_body,
        grid=(SEQ // FFN_TS, D_FF // FFN_TF),
        in_specs=[
            pl.BlockSpec((FFN_TS, D_MODEL), lambda i, f: (i, 0)),
            pl.BlockSpec((1, D_MODEL), lambda i, f: (0, 0)),
            pl.BlockSpec((D_MODEL, FFN_TF), lambda i, f: (0, f)),
            pl.BlockSpec((D_MODEL, FFN_TF), lambda i, f: (0, f)),
            pl.BlockSpec((FFN_TF, D_MODEL), lambda i, f: (f, 0)),
            pl.BlockSpec((1, D_MODEL), lambda i, f: (0, 0)),
        ],
        out_specs=pl.BlockSpec((FFN_TS, D_MODEL), lambda i, f: (i, 0)),
        out_shape=jax.ShapeDtypeStruct((SEQ, D_MODEL), F32),
        scratch_shapes=[pltpu.VMEM((FFN_TS, D_MODEL), BF16)],
        compiler_params=_params("parallel", "arbitrary"),
        name="ffn",
    )(h1, ln2, wg, wu, wd, fnw)


def _retention_tables():
    H, Dh, C = RET_HEADS, RET_HD, RET_C
    inv_freq = ROPE_BASE ** (-np.arange(0, Dh, 2, dtype=np.float64) / Dh)
    inv2 = np.concatenate([inv_freq, inv_freq])
    off = np.arange(MIX_TS, dtype=np.float64)[:, None] * inv2[None, :]
    base = (np.arange(N_BLK, dtype=np.float64) * MIX_TS)[:, None, None] * inv2[None, None, :]
    log_gamma = np.log1p(-np.exp2(-5.0 - np.arange(H, dtype=np.float64)))
    idx = np.arange(C)
    diff = idx[:, None] - idx[None, :]
    decay = np.where(diff >= 0, np.exp(log_gamma[:, None, None] * np.maximum(diff, 0)[None]), 0.0)
    zeta = np.exp(log_gamma[:, None] * (C - 1 - idx)[None, :])
    xi = np.exp(log_gamma[:, None] * (idx + 1)[None, :])
    gc = np.exp(log_gamma * C)
    per_head = lambda t: np.broadcast_to(t[:, :, None], (H, C, Dh))
    tables = (np.cos(off), np.sin(off), np.cos(base), np.sin(base), decay, per_head(xi), per_head(zeta),
              np.broadcast_to(gc[:, None, None], (H, 1, Dh)))
    return tuple(jnp.asarray(np.ascontiguousarray(t, dtype=np.float32)) for t in tables)


def kernel(x, ln1_w, w_in, conv_w, conv_b, gate_a_w, gate_a_b, gate_x_w, gate_x_b, lru_lambda, ret_gn_w,
           w_out, ln2_w, w_ffn_gate, w_ffn_up, w_ffn_down, final_norm_w):
    x2 = x.reshape(SEQ, D_MODEL)
    row = lambda v: v.reshape(1, -1)
    by_lb = lambda v: v.reshape(-1, 1, LANES)
    wcat = jnp.concatenate([gate_a_w[0], gate_x_w[0]], axis=-1).astype(BF16)
    cw = conv_w[0].reshape(CONV_W, N_LRU_LB, LANES).transpose(1, 0, 2)
    y, wg, wu, wd, wo = _mix(
        x2, row(ln1_w[0]), w_in[0].astype(BF16),
        cw, by_lb(conv_b[0]), wcat, by_lb(gate_a_b[0]), by_lb(gate_x_b[0]), by_lb(lru_lambda[0]),
        *_retention_tables(), by_lb(ret_gn_w[0]),
        w_ffn_gate[0], w_ffn_up[0], w_ffn_down[0], w_out[0])
    h1 = _outproj(x2, y, wo)
    out = _ffn(h1, row(ln2_w[0]), wg, wu, wd, row(final_norm_w))
    return out.reshape(1, SEQ, D_MODEL)
```

```python
import jax
import jax.numpy as jnp
import numpy as np
from jax import lax
from jax.experimental import pallas as pl
from jax.experimental.pallas import tpu as pltpu

D_MODEL = 2048
SEQ = 8192
D_LRU = 1024
D_RET = 1024
CONV_W = 4
RET_HEADS = 8
RET_HD = 128
ROPE_BASE = 10000.0
D_FF = 5632
D_IN = 6144
EPS = 1e-6
GELU_C = 0.7978845608028654

SUBLANES = 8
LANES = 128
VMEM_LIMIT_BYTES = 56 * 1024 * 1024

F32 = jnp.float32
BF16 = jnp.bfloat16

NORM_ROWS = 128
MIX_TS = 512
MIX_J = 8
MIX_NW = D_IN // MIX_J
MIX_LB = MIX_NW // LANES
N_BLK = SEQ // MIX_TS
N_LRU_LB = D_LRU // LANES
N_REST_LB = (D_IN - D_LRU) // LANES
LRU_R = 128
LRU_SUB = MIX_TS // LRU_R
RET_C = 256
OUT_TS = 512
FFN_TS, FFN_TF = 512, 512


def _rmsnorm_rows(x, w):
    ms = jnp.mean(x * x, axis=-1, keepdims=True)
    return (x * lax.rsqrt(ms + EPS)) * w


def _sigmoid(x):
    return 0.5 * jnp.tanh(0.5 * x) + 0.5


def _params(*sem):
    return pltpu.CompilerParams(dimension_semantics=sem, vmem_limit_bytes=VMEM_LIMIT_BYTES)


def _lru_sub_block(r0, hc, slot, m, rd_lru, rd_rest, ysc_ref, sa_ref, sb_ref, cw, cb, wcat, ba, bx, lsl4):
    nt = LRU_R // SUBLANES
    base = r0 + SUBLANES
    xc = cb + rd_lru[m, pl.ds(base - 3, LRU_R), :] * cw[0:1]
    xc = xc + rd_lru[m, pl.ds(base - 2, LRU_R), :] * cw[1:2]
    xc = xc + rd_lru[m, pl.ds(base - 1, LRU_R), :] * cw[2:3]
    xc = xc + rd_lru[m, pl.ds(base, LRU_R), :] * cw[3:4]
    gates = jnp.dot(xc.astype(BF16), wcat, preferred_element_type=F32)
    tr = jnp.tanh(0.5 * (gates[:, :LANES] + ba))
    ti = jnp.tanh(0.5 * (gates[:, LANES:] + bx))
    log_a = (tr + 1.0) * lsl4
    a = jnp.exp(log_a)
    v = -jnp.tanh(log_a)
    coef = jnp.where(v > 0.0, v * lax.rsqrt((v + v) * (1.0 + v)), 0.0)
    b = coef * ((ti + 1.0) * xc)
    a3 = a.reshape(nt, SUBLANES, LANES)
    b3 = b.reshape(nt, SUBLANES, LANES)
    for s in (1, 2, 4):
        sa_ref[slot, :, SUBLANES:2 * SUBLANES, :] = a3
        sb_ref[slot, :, SUBLANES:2 * SUBLANES, :] = b3
        a_sh = sa_ref[slot, :, SUBLANES - s:2 * SUBLANES - s, :]
        b_sh = sb_ref[slot, :, SUBLANES - s:2 * SUBLANES - s, :]
        b3 = a3 * b_sh + b3
        a3 = a3 * a_sh
    tiles = []
    for t in range(nt):
        ht = a3[t] * hc + b3[t]
        hc = ht[SUBLANES - 1:SUBLANES, :]
        tiles.append(ht)
    h = jnp.concatenate(tiles, axis=0)
    g = rd_rest[m, pl.ds(r0, LRU_R), :].astype(F32)
    tg = jnp.tanh(g * (GELU_C + (GELU_C * 0.044715) * (g * g)))
    hg = 0.5 * g
    ysc_ref[m, pl.ds(r0, LRU_R), :] = (h * (hg * tg + hg)).astype(BF16)
    return hc


def _mix_body(x_ref, ln_ref, w_ref,
              cw_ref, cb_ref, wg_ref, ba_ref, bx_ref, lam_ref,
              cr_ref, sr_ref, cbase_ref, sbase_ref, dec_ref, xi_ref, zeta_ref, gc_ref, gnw_ref,
              fg_ref, fu_ref, fd_ref, fo_ref,
              y_ref, bg_ref, bu_ref, bd_ref, bo_ref,
              u_ref, plru_a, plru_b, prest_a, prest_b, ysc_ref, sa_ref, sb_ref, h_ref, state_ref,
              cq_ref, sq_ref, ck_ref, sk_ref):
    i = pl.program_id(0)
    j = pl.program_id(1)
    m = j
    nt = LRU_R // SUBLANES

    bg_ref[...] = fg_ref[...].astype(BF16)
    bu_ref[...] = fu_ref[...].astype(BF16)
    bd_ref[...] = fd_ref[...].astype(BF16)
    bo_ref[...] = fo_ref[...].astype(BF16)

    @pl.when(jnp.logical_and(i == 0, j == 0))
    def _():
        h_ref[...] = jnp.zeros_like(h_ref)
        state_ref[...] = jnp.zeros_like(state_ref)
        plru_a[:, 0:SUBLANES, :] = jnp.zeros((N_LRU_LB + 1, SUBLANES, LANES), F32)

    @pl.when(j == 0)
    def _():
        def step(r, carry):
            r0 = pl.multiple_of(r * NORM_ROWS, NORM_ROWS)
            x = x_ref[pl.ds(r0, NORM_ROWS), :]
            u_ref[pl.ds(r0, NORM_ROWS), :] = _rmsnorm_rows(x, ln_ref[...]).astype(BF16)
            return carry

        lax.fori_loop(0, MIX_TS // NORM_ROWS, step, 0)

        blk = jnp.maximum(i - 1, 0)
        cb_, sb_ = cbase_ref[blk], sbase_ref[blk]
        cos = cb_ * cr_ref[...] - sb_ * sr_ref[...]
        sin = sb_ * cr_ref[...] + cb_ * sr_ref[...]
        lane = lax.broadcasted_iota(jnp.int32, (1, RET_HD), 1)
        sin = jnp.where(lane < RET_HD // 2, -sin, sin)
        cq_ref[...] = cos
        sq_ref[...] = sin
        ck_ref[...] = cos * (RET_HD ** -0.5)
        sk_ref[...] = sin * (RET_HD ** -0.5)

    def head_groups(rd_lru, rd_rest, wr_lru):
        sa_ref[:, :, 0:SUBLANES, :] = jnp.ones((LRU_SUB, nt, SUBLANES, LANES), F32)
        sb_ref[:, :, 0:SUBLANES, :] = jnp.zeros((LRU_SUB, nt, SUBLANES, LANES), F32)
        lam = lam_ref[m]
        lsl4 = -4.0 * (jnp.maximum(-lam, 0.0) + jnp.log1p(jnp.exp(-jnp.abs(lam))))
        hc = h_ref[m][0:1, :]
        for slot in range(LRU_SUB):
            hc = _lru_sub_block(slot * LRU_R, hc, slot, m, rd_lru, rd_rest, ysc_ref, sa_ref, sb_ref,
                                cw_ref[m], cb_ref[m], wg_ref[m], ba_ref[m], bx_ref[m], lsl4)
        h_ref[m] = jnp.broadcast_to(hc, (SUBLANES, LANES))
        wr_lru[m, 0:SUBLANES, :] = rd_lru[m, pl.ds(MIX_TS, SUBLANES), :]

        state = state_ref[m]
        for c in range(MIX_TS // RET_C):
            rows = slice(c * RET_C, (c + 1) * RET_C)
            q = rd_rest[8 + m, rows, :].astype(F32)
            k = rd_rest[16 + m, rows, :].astype(F32)
            vb = rd_rest[24 + m, rows, :]
            g = rd_rest[32 + m, rows, :].astype(F32)
            qr = q * cq_ref[rows, :] + pltpu.roll(q, RET_HD // 2, 1) * sq_ref[rows, :]
            kr = k * ck_ref[rows, :] + pltpu.roll(k, RET_HD // 2, 1) * sk_ref[rows, :]
            scores = lax.dot_general(qr.astype(BF16), kr.astype(BF16), (((1,), (1,)), ((), ())),
                                     preferred_element_type=F32)
            p = (scores * dec_ref[m]).astype(BF16)
            lhs = jnp.concatenate([p, (qr * xi_ref[m]).astype(BF16)], axis=1)
            rhs = jnp.concatenate([vb, state.astype(BF16)], axis=0)
            o = jnp.dot(lhs, rhs, preferred_element_type=F32)
            kz = (kr * zeta_ref[m]).astype(BF16)
            kv = lax.dot_general(kz, vb, (((0,), (0,)), ((), ())), preferred_element_type=F32)
            state = state * gc_ref[m] + kv
            mu = jnp.mean(o, axis=-1, keepdims=True)
            d = o - mu
            var = jnp.mean(d * d, axis=-1, keepdims=True)
            on = (d * lax.rsqrt(var + EPS)) * gnw_ref[m]
            ysc_ref[N_LRU_LB + m, rows, :] = ((g * _sigmoid(g)) * on).astype(BF16)
        state_ref[m] = state

    def projection(wr_lru, wr_rest):
        res = jnp.dot(u_ref[...], w_ref[...], preferred_element_type=F32)
        for kk in range(MIX_LB):
            lb = j * MIX_LB + kk
            blk = res[:, kk * LANES:(kk + 1) * LANES]
            wr_lru[jnp.minimum(lb, N_LRU_LB), pl.ds(SUBLANES, MIX_TS), :] = blk
            wr_rest[jnp.where(lb >= N_LRU_LB, lb - N_LRU_LB, N_REST_LB), :, :] = blk.astype(BF16)

    first, last, even = i == 0, i == N_BLK, lax.rem(i, 2) == 0
    mid = jnp.logical_not(jnp.logical_or(first, last))

    @pl.when(first)
    def _():
        projection(plru_a, prest_a)

    @pl.when(jnp.logical_and(mid, even))
    def _():
        head_groups(plru_b, prest_b, plru_a)
        projection(plru_a, prest_a)

    @pl.when(jnp.logical_and(mid, jnp.logical_not(even)))
    def _():
        head_groups(plru_a, prest_a, plru_b)
        projection(plru_b, prest_b)

    @pl.when(last)
    def _():
        head_groups(plru_b, prest_b, plru_a)

    @pl.when(jnp.logical_and(j == MIX_J - 1, jnp.logical_not(first)))
    def _():
        for lb in range(2 * N_LRU_LB):
            y_ref[:, lb * LANES:(lb + 1) * LANES] = ysc_ref[lb]


def _cast_spec(rows, cols, n_chunks):
    return pl.BlockSpec((rows, cols), lambda i, j: (jnp.minimum(i * MIX_J + j, n_chunks - 1), 0))


def _mix(x, ln, w_in, cw, cb, wcat, ba, bx, lam, cr, sr, cbase, sbase, decay, xi, zeta, gc, gnw,
         w_gate, w_up, w_down, w_out):
    assert N_BLK % 2 == 0
    prev_blk = lambda i, j: (jnp.maximum(i - 1, 0), 0)
    const2 = lambda a: pl.BlockSpec(a.shape, lambda i, j: (0, 0))
    const3 = lambda a: pl.BlockSpec(a.shape, lambda i, j: (0, 0, 0))
    nt = LRU_R // SUBLANES
    n_steps = N_BLK * MIX_J
    up_rows = D_MODEL // n_steps
    down_rows = 4 * up_rows
    cast_specs = [
        _cast_spec(up_rows, D_FF, D_MODEL // up_rows),
        _cast_spec(up_rows, D_FF, D_MODEL // up_rows),
        _cast_spec(down_rows, D_MODEL, D_FF // down_rows),
        _cast_spec(up_rows, D_MODEL, D_MODEL // up_rows),
    ]
    lru_buf = pltpu.VMEM((N_LRU_LB + 1, SUBLANES + MIX_TS, LANES), F32)
    rest_buf = pltpu.VMEM((N_REST_LB + 1, MIX_TS, LANES), BF16)
    return pl.pallas_call(
        _mix_body,
        grid=(N_BLK + 1, MIX_J),
        in_specs=[
            pl.BlockSpec((MIX_TS, D_MODEL), lambda i, j: (jnp.minimum(i, N_BLK - 1), 0)),
            pl.BlockSpec((1, D_MODEL), lambda i, j: (0, 0)),
            pl.BlockSpec((D_MODEL, MIX_NW), lambda i, j: (0, j)),
            const3(cw), const3(cb), const3(wcat), const3(ba), const3(bx), const3(lam),
            const2(cr), const2(sr), const3(cbase), const3(sbase),
            const3(decay), const3(xi), const3(zeta), const3(gc), const3(gnw),
            *cast_specs,
        ],
        out_specs=[pl.BlockSpec((MIX_TS, D_LRU + D_RET), prev_blk), *cast_specs],
        out_shape=[
            jax.ShapeDtypeStruct((SEQ, D_LRU + D_RET), BF16),
            jax.ShapeDtypeStruct(w_gate.shape, BF16),
            jax.ShapeDtypeStruct(w_up.shape, BF16),
            jax.ShapeDtypeStruct(w_down.shape, BF16),
            jax.ShapeDtypeStruct(w_out.shape, BF16),
        ],
        scratch_shapes=[
            pltpu.VMEM((MIX_TS, D_MODEL), BF16),
            lru_buf, lru_buf, rest_buf, rest_buf,
            pltpu.VMEM((2 * N_LRU_LB, MIX_TS, LANES), BF16),
            pltpu.VMEM((LRU_SUB, nt, 2 * SUBLANES, LANES), F32),
            pltpu.VMEM((LRU_SUB, nt, 2 * SUBLANES, LANES), F32),
            pltpu.VMEM((N_LRU_LB, SUBLANES, LANES), F32),
            pltpu.VMEM((RET_HEADS, RET_HD, RET_HD), F32),
            pltpu.VMEM((MIX_TS, RET_HD), F32),
            pltpu.VMEM((MIX_TS, RET_HD), F32),
            pltpu.VMEM((MIX_TS, RET_HD), F32),
            pltpu.VMEM((MIX_TS, RET_HD), F32),
        ],
        compiler_params=_params("arbitrary", "arbitrary"),
        name="mix",
    )(x, ln, w_in, cw, cb, wcat, ba, bx, lam, cr, sr, cbase, sbase, decay, xi, zeta, gc, gnw,
      w_gate, w_up, w_down, w_out)


def _outproj_body(x_ref, y_ref, wo_ref, o_ref):
    o_ref[...] = x_ref[...] + jnp.dot(y_ref[...], wo_ref[...], preferred_element_type=F32)


def _outproj(x, y, w_out):
    return pl.pallas_call(
        _outproj_body,
        grid=(SEQ // OUT_TS,),
        in_specs=[
            pl.BlockSpec((OUT_TS, D_MODEL), lambda i: (i, 0)),
            pl.BlockSpec((OUT_TS, D_LRU + D_RET), lambda i: (i, 0)),
            pl.BlockSpec((D_MODEL, D_MODEL), lambda i: (0, 0)),
        ],
        out_specs=pl.BlockSpec((OUT_TS, D_MODEL), lambda i: (i, 0)),
        out_shape=jax.ShapeDtypeStruct((SEQ, D_MODEL), F32),
        compiler_params=_params("parallel"),
        name="outproj",
    )(x, y, w_out)


def _ffn_body(h_ref, ln2_ref, wg_ref, wu_ref, wd_ref, fnw_ref, o_ref, u_ref):
    f = pl.program_id(1)

    @pl.when(f == 0)
    def _():
        def step(c, carry):
            r0 = pl.multiple_of(c * NORM_ROWS, NORM_ROWS)
            hrows = h_ref[pl.ds(r0, NORM_ROWS), :]
            u_ref[pl.ds(r0, NORM_ROWS), :] = _rmsnorm_rows(hrows, ln2_ref[...]).astype(BF16)
            o_ref[pl.ds(r0, NORM_ROWS), :] = hrows
            return carry

        lax.fori_loop(0, FFN_TS // NORM_ROWS, step, 0)

    u = u_ref[...]
    gate = jnp.dot(u, wg_ref[...], preferred_element_type=F32)
    up = jnp.dot(u, wu_ref[...], preferred_element_type=F32)
    act = ((gate * _sigmoid(gate)) * up).astype(BF16)
    o_ref[...] += jnp.dot(act, wd_ref[...], preferred_element_type=F32)

    @pl.when(f == pl.num_programs(1) - 1)
    def _():
        def step(c, carry):
            r0 = pl.multiple_of(c * NORM_ROWS, NORM_ROWS)
            rows = pl.ds(r0, NORM_ROWS)
            o_ref[rows, :] = _rmsnorm_rows(o_ref[rows, :], fnw_ref[...])
            return carry

        lax.fori_loop(0, FFN_TS // NORM_ROWS, step, 0)


def _ffn(h1, ln2, wg, wu, wd, fnw):
    return pl.pallas_call(
        _ffn_body,
        grid=(SEQ // FFN_TS, D_FF // FFN_TF),
        in_specs=[
            pl.BlockSpec((FFN_TS, D_MODEL), lambda i, f: (i, 0)),
            pl.BlockSpec((1, D_MODEL), lambda i, f: (0, 0)),
            pl.BlockSpec((D_MODEL, FFN_TF), lambda i, f: (0, f)),
            pl.BlockSpec((D_MODEL, FFN_TF), lambda i, f: (0, f)),
            pl.BlockSpec((FFN_TF, D_MODEL), lambda i, f: (f, 0)),
            pl.BlockSpec((1, D_MODEL), lambda i, f: (0, 0)),
        ],
        out_specs=pl.BlockSpec((FFN_TS, D_MODEL), lambda i, f: (i, 0)),
        out_shape=jax.ShapeDtypeStruct((SEQ, D_MODEL), F32),
        scratch_shapes=[pltpu.VMEM((FFN_TS, D_MODEL), BF16)],
        compiler_params=_params("parallel", "arbitrary"),
        name="ffn",
    )(h1, ln2, wg, wu, wd, fnw)


def _retention_tables():
    H, Dh, C = RET_HEADS, RET_HD, RET_C
    inv_freq = ROPE_BASE ** (-np.arange(0, Dh, 2, dtype=np.float64) / Dh)
    inv2 = np.concatenate([inv_freq, inv_freq])
    off = np.arange(MIX_TS, dtype=np.float64)[:, None] * inv2[None, :]
    base = (np.arange(N_BLK, dtype=np.float64) * MIX_TS)[:, None, None] * inv2[None, None, :]
    log_gamma = np.log1p(-np.exp2(-5.0 - np.arange(H, dtype=np.float64)))
    idx = np.arange(C)
    diff = idx[:, None] - idx[None, :]
    decay = np.where(diff >= 0, np.exp(log_gamma[:, None, None] * np.maximum(diff, 0)[None]), 0.0)
    zeta = np.exp(log_gamma[:, None] * (C - 1 - idx)[None, :])
    xi = np.exp(log_gamma[:, None] * (idx + 1)[None, :])
    gc = np.exp(log_gamma * C)
    per_head = lambda t: np.broadcast_to(t[:, :, None], (H, C, Dh))
    tables = (np.cos(off), np.sin(off), np.cos(base), np.sin(base), decay, per_head(xi), per_head(zeta),
              np.broadcast_to(gc[:, None, None], (H, 1, Dh)))
    return tuple(jnp.asarray(np.ascontiguousarray(t, dtype=np.float32)) for t in tables)


def kernel(x, ln1_w, w_in, conv_w, conv_b, gate_a_w, gate_a_b, gate_x_w, gate_x_b, lru_lambda, ret_gn_w,
           w_out, ln2_w, w_ffn_gate, w_ffn_up, w_ffn_down, final_norm_w):
    x2 = x.reshape(SEQ, D_MODEL)
    row = lambda v: v.reshape(1, -1)
    by_lb = lambda v: v.reshape(-1, 1, LANES)
    wcat = jnp.concatenate([gate_a_w[0], gate_x_w[0]], axis=-1).astype(BF16)
    cw = conv_w[0].reshape(CONV_W, N_LRU_LB, LANES).transpose(1, 0, 2)
    y, wg, wu, wd, wo = _mix(
        x2, row(ln1_w[0]), w_in[0].astype(BF16),
        cw, by_lb(conv_b[0]), wcat, by_lb(gate_a_b[0]), by_lb(gate_x_b[0]), by_lb(lru_lambda[0]),
        *_retention_tables(), by_lb(ret_gn_w[0]),
        w_ffn_gate[0], w_ffn_up[0], w_ffn_down[0], w_out[0])
    h1 = _outproj(x2, y, wo)
    out = _ffn(h1, row(ln2_w[0]), wg, wu, wd, row(final_norm_w))
    return out.reshape(1, SEQ, D_MODEL)
```

```python
import jax
import jax.numpy as jnp
import numpy as np
from jax import lax
from jax.experimental import pallas as pl
from jax.experimental.pallas import tpu as pltpu

D_MODEL = 2048
SEQ = 8192
D_LRU = 1024
D_RET = 1024
CONV_W = 4
RET_HEADS = 8
RET_HD = 128
ROPE_BASE = 10000.0
D_FF = 5632
D_IN = 6144
EPS = 1e-6
GELU_C = 0.7978845608028654

SUBLANES = 8
LANES = 128
VMEM_LIMIT_BYTES = 56 * 1024 * 1024

F32 = jnp.float32
BF16 = jnp.bfloat16

NORM_ROWS = 128
MIX_TS = 512
MIX_J = 8
MIX_NW = D_IN // MIX_J
MIX_LB = MIX_NW // LANES
N_BLK = SEQ // MIX_TS
N_LRU_LB = D_LRU // LANES
N_REST_LB = (D_IN - D_LRU) // LANES
LRU_R = 128
LRU_SUB = MIX_TS // LRU_R
RET_C = 256
OUT_TS = 512
FFN_TS, FFN_TF = 512, 512
U_CONV_B, U_GATE_A_B, U_GATE_X_B, U_LAMBDA, U_CHUNK_DECAY, U_GN_W = 4, 5, 6, 7, 8, 9
UNIT_ROWS = 16


def _rmsnorm_rows(x, w):
    ms = jnp.mean(x * x, axis=-1, keepdims=True)
    return (x * lax.rsqrt(ms + EPS)) * w


def _sigmoid(x):
    return 0.5 * jnp.tanh(0.5 * x) + 0.5


def _params(*sem):
    return pltpu.CompilerParams(dimension_semantics=sem, vmem_limit_bytes=VMEM_LIMIT_BYTES)


def _lru_sub_block(r0, hc, slot, m, rd_lru, rd_rest, ysc_ref, sa_ref, sb_ref, cw, cb, wcat, ba, bx, lsl4):
    nt = LRU_R // SUBLANES
    base = r0 + SUBLANES
    xc = cb + rd_lru[m, pl.ds(base - 3, LRU_R), :] * cw[0:1]
    xc = xc + rd_lru[m, pl.ds(base - 2, LRU_R), :] * cw[1:2]
    xc = xc + rd_lru[m, pl.ds(base - 1, LRU_R), :] * cw[2:3]
    xc = xc + rd_lru[m, pl.ds(base, LRU_R), :] * cw[3:4]
    gates = jnp.dot(xc.astype(BF16), wcat, preferred_element_type=F32)
    tr = jnp.tanh(0.5 * (gates[:, :LANES] + ba))
    ti = jnp.tanh(0.5 * (gates[:, LANES:] + bx))
    log_a = (tr + 1.0) * lsl4
    a = jnp.exp(log_a)
    v = -jnp.tanh(log_a)
    coef = jnp.where(v > 0.0, v * lax.rsqrt((v + v) * (1.0 + v)), 0.0)
    b = coef * ((ti + 1.0) * xc)
    a3 = a.reshape(nt, SUBLANES, LANES)
    b3 = b.reshape(nt, SUBLANES, LANES)
    for s in (1, 2, 4):
        sa_ref[slot, :, SUBLANES:2 * SUBLANES, :] = a3
        sb_ref[slot, :, SUBLANES:2 * SUBLANES, :] = b3
        a_sh = sa_ref[slot, :, SUBLANES - s:2 * SUBLANES - s, :]
        b_sh = sb_ref[slot, :, SUBLANES - s:2 * SUBLANES - s, :]
        b3 = a3 * b_sh + b3
        a3 = a3 * a_sh
    tiles = []
    for t in range(nt):
        ht = a3[t] * hc + b3[t]
        hc = ht[SUBLANES - 1:SUBLANES, :]
        tiles.append(ht)
    h = jnp.concatenate(tiles, axis=0)
    g = rd_rest[m, pl.ds(r0, LRU_R), :].astype(F32)
    tg = jnp.tanh(g * (GELU_C + (GELU_C * 0.044715) * (g * g)))
    hg = 0.5 * g
    ysc_ref[m, pl.ds(r0, LRU_R), :] = (h * (hg * tg + hg)).astype(BF16)
    return hc


def _mix_body(x_ref, ln_ref, w_ref, unit_ref, wg_ref, rot_ref, base_ref, dec_ref, xz_ref,
              fg_ref, fu_ref, fd_ref, fo_ref,
              y_ref, bg_ref, bu_ref, bd_ref, bo_ref,
              u_ref, plru_a, plru_b, prest_a, prest_b, ysc_ref, sa_ref, sb_ref, h_ref, state_ref,
              cq_ref, sq_ref, ck_ref, sk_ref):
    i = pl.program_id(0)
    j = pl.program_id(1)
    m = j
    nt = LRU_R // SUBLANES

    for f in range(D_FF // FFN_TF):
        bg_ref[f] = fg_ref[:, f * FFN_TF:(f + 1) * FFN_TF].astype(BF16)
        bu_ref[f] = fu_ref[:, f * FFN_TF:(f + 1) * FFN_TF].astype(BF16)
    bd_ref[...] = fd_ref[...].astype(BF16)
    bo_ref[...] = fo_ref[...].astype(BF16)

    @pl.when(jnp.logical_and(i == 0, j == 0))
    def _():
        h_ref[...] = jnp.zeros_like(h_ref)
        state_ref[...] = jnp.zeros_like(state_ref)
        plru_a[:, 0:SUBLANES, :] = jnp.zeros((N_LRU_LB, SUBLANES, LANES), F32)

    @pl.when(j == 0)
    def _():
        def step(r, carry):
            r0 = pl.multiple_of(r * NORM_ROWS, NORM_ROWS)
            x = x_ref[pl.ds(r0, NORM_ROWS), :]
            u_ref[pl.ds(r0, NORM_ROWS), :] = _rmsnorm_rows(x, ln_ref[...]).astype(BF16)
            return carry

        lax.fori_loop(0, MIX_TS // NORM_ROWS, step, 0)

        blk = jnp.maximum(i - 1, 0)
        cb_, sb_ = base_ref[blk][0:1], base_ref[blk][1:2]
        cos = cb_ * rot_ref[0] - sb_ * rot_ref[1]
        sin = sb_ * rot_ref[0] + cb_ * rot_ref[1]
        lane = lax.broadcasted_iota(jnp.int32, (1, RET_HD), 1)
        sin = jnp.where(lane < RET_HD // 2, -sin, sin)
        cq_ref[...] = cos
        sq_ref[...] = sin
        ck_ref[...] = cos * (RET_HD ** -0.5)
        sk_ref[...] = sin * (RET_HD ** -0.5)

    def head_groups(rd_lru, rd_rest, wr_lru):
        sa_ref[:, :, 0:SUBLANES, :] = jnp.ones((LRU_SUB, nt, SUBLANES, LANES), F32)
        sb_ref[:, :, 0:SUBLANES, :] = jnp.zeros((LRU_SUB, nt, SUBLANES, LANES), F32)
        unit = unit_ref[m]
        cw, cb = unit[0:CONV_W], unit[U_CONV_B:U_CONV_B + 1]
        ba, bx = unit[U_GATE_A_B:U_GATE_A_B + 1], unit[U_GATE_X_B:U_GATE_X_B + 1]
        lam = unit[U_LAMBDA:U_LAMBDA + 1]
        gc, gnw = unit[U_CHUNK_DECAY:U_CHUNK_DECAY + 1], unit[U_GN_W:U_GN_W + 1]
        lsl4 = -4.0 * (jnp.maximum(-lam, 0.0) + jnp.log1p(jnp.exp(-jnp.abs(lam))))
        hc = h_ref[m][0:1, :]
        for slot in range(LRU_SUB):
            hc = _lru_sub_block(slot * LRU_R, hc, slot, m, rd_lru, rd_rest, ysc_ref, sa_ref, sb_ref,
                                cw, cb, wg_ref[m], ba, bx, lsl4)
        h_ref[m] = jnp.broadcast_to(hc, (SUBLANES, LANES))
        wr_lru[m, 0:SUBLANES, :] = rd_lru[m, pl.ds(MIX_TS, SUBLANES), :]

        state = state_ref[m]
        for c in range(MIX_TS // RET_C):
            rows = slice(c * RET_C, (c + 1) * RET_C)
            q = rd_rest[8 + m, rows, :].astype(F32)
            k = rd_rest[16 + m, rows, :].astype(F32)
            vb = rd_rest[24 + m, rows, :]
            g = rd_rest[32 + m, rows, :].astype(F32)
            qr = q * cq_ref[rows, :] + pltpu.roll(q, RET_HD // 2, 1) * sq_ref[rows, :]
            kr = k * ck_ref[rows, :] + pltpu.roll(k, RET_HD // 2, 1) * sk_ref[rows, :]
            scores = lax.dot_general(qr.astype(BF16), kr.astype(BF16), (((1,), (1,)), ((), ())),
                                     preferred_element_type=F32)
            p = (scores * dec_ref[m]).astype(BF16)
            lhs = jnp.concatenate([p, (qr * xz_ref[0, m]).astype(BF16)], axis=1)
            rhs = jnp.concatenate([vb, state.astype(BF16)], axis=0)
            o = jnp.dot(lhs, rhs, preferred_element_type=F32)
            kz = (kr * xz_ref[1, m]).astype(BF16)
            kv = lax.dot_general(kz, vb, (((0,), (0,)), ((), ())), preferred_element_type=F32)
            state = state * gc + kv
            mu = jnp.mean(o, axis=-1, keepdims=True)
            d = o - mu
            var = jnp.mean(d * d, axis=-1, keepdims=True)
            on = (d * lax.rsqrt(var + EPS)) * gnw
            ysc_ref[N_LRU_LB + m, rows, :] = ((g * _sigmoid(g)) * on).astype(BF16)
        state_ref[m] = state

    def projection(wr_lru, wr_rest):
        res = jnp.dot(u_ref[...], w_ref[...], preferred_element_type=F32)
        wr_lru[j, pl.ds(SUBLANES, MIX_TS), :] = res[:, 0:LANES]
        for kk in range(1, MIX_LB):
            wr_rest[(kk - 1) * MIX_J + j, :, :] = res[:, kk * LANES:(kk + 1) * LANES].astype(BF16)

    first, last, even = i == 0, i == N_BLK, lax.rem(i, 2) == 0
    mid = jnp.logical_not(jnp.logical_or(first, last))

    @pl.when(first)
    def _():
        projection(plru_a, prest_a)

    @pl.when(jnp.logical_and(mid, even))
    def _():
        head_groups(plru_b, prest_b, plru_a)
        projection(plru_a, prest_a)

    @pl.when(jnp.logical_and(mid, jnp.logical_not(even)))
    def _():
        head_groups(plru_a, prest_a, plru_b)
        projection(plru_b, prest_b)

    @pl.when(last)
    def _():
        head_groups(plru_b, prest_b, plru_a)

    @pl.when(jnp.logical_and(j == MIX_J - 1, jnp.logical_not(first)))
    def _():
        for lb in range(2 * N_LRU_LB):
            y_ref[:, lb * LANES:(lb + 1) * LANES] = ysc_ref[lb]


def _mix(x, ln, w_slabs, unit_tab, wcat, rot, base, decay, xz, w_gate, w_up, w_down, w_out):
    assert N_BLK % 2 == 0
    prev_blk = lambda i, j: (jnp.maximum(i - 1, 0), 0)
    const = lambda a: pl.BlockSpec(a.shape, lambda i, j: (0,) * a.ndim)
    chunk = lambda i, j, n: jnp.minimum(i * MIX_J + j, n - 1)
    nt = LRU_R // SUBLANES
    n_steps = N_BLK * MIX_J
    up_rows = D_MODEL // n_steps
    down_rows = 4 * up_rows
    n_up, n_down, n_ff = D_MODEL // up_rows, D_FF // down_rows, D_FF // FFN_TF
    up_in = pl.BlockSpec((up_rows, D_FF), lambda i, j: (chunk(i, j, n_up), 0))
    up_out = pl.BlockSpec((n_ff, up_rows, FFN_TF), lambda i, j: (0, chunk(i, j, n_up), 0))
    down_io = pl.BlockSpec((down_rows, D_MODEL), lambda i, j: (chunk(i, j, n_down), 0))
    out_io = pl.BlockSpec((up_rows, D_MODEL), lambda i, j: (chunk(i, j, n_up), 0))
    lru_buf = pltpu.VMEM((N_LRU_LB, SUBLANES + MIX_TS, LANES), F32)
    rest_buf = pltpu.VMEM((N_REST_LB, MIX_TS, LANES), BF16)
    return pl.pallas_call(
        _mix_body,
        grid=(N_BLK + 1, MIX_J),
        in_specs=[
            pl.BlockSpec((MIX_TS, D_MODEL), lambda i, j: (jnp.minimum(i, N_BLK - 1), 0)),
            pl.BlockSpec((1, D_MODEL), lambda i, j: (0, 0)),
            pl.BlockSpec((None, D_MODEL, MIX_NW), lambda i, j: (j, 0, 0)),
            const(unit_tab), const(wcat), const(rot), const(base), const(decay), const(xz),
            up_in, up_in, down_io, out_io,
        ],
        out_specs=[pl.BlockSpec((MIX_TS, D_LRU + D_RET), prev_blk), up_out, up_out, down_io, out_io],
        out_shape=[
            jax.ShapeDtypeStruct((SEQ, D_LRU + D_RET), BF16),
            jax.ShapeDtypeStruct((n_ff, D_MODEL, FFN_TF), BF16),
            jax.ShapeDtypeStruct((n_ff, D_MODEL, FFN_TF), BF16),
            jax.ShapeDtypeStruct(w_down.shape, BF16),
            jax.ShapeDtypeStruct(w_out.shape, BF16),
        ],
        scratch_shapes=[
            pltpu.VMEM((MIX_TS, D_MODEL), BF16),
            lru_buf, lru_buf, rest_buf, rest_buf,
            pltpu.VMEM((2 * N_LRU_LB, MIX_TS, LANES), BF16),
            pltpu.VMEM((LRU_SUB, nt, 2 * SUBLANES, LANES), F32),
            pltpu.VMEM((LRU_SUB, nt, 2 * SUBLANES, LANES), F32),
            pltpu.VMEM((N_LRU_LB, SUBLANES, LANES), F32),
            pltpu.VMEM((RET_HEADS, RET_HD, RET_HD), F32),
            pltpu.VMEM((MIX_TS, RET_HD), F32),
            pltpu.VMEM((MIX_TS, RET_HD), F32),
            pltpu.VMEM((MIX_TS, RET_HD), F32),
            pltpu.VMEM((MIX_TS, RET_HD), F32),
        ],
        compiler_params=_params("arbitrary", "arbitrary"),
        name="mix",
    )(x, ln, w_slabs, unit_tab, wcat, rot, base, decay, xz, w_gate, w_up, w_down, w_out)


def _outproj_body(x_ref, y_ref, wo_ref, o_ref):
    o_ref[...] = x_ref[...] + jnp.dot(y_ref[...], wo_ref[...], preferred_element_type=F32)


def _outproj(x, y, w_out):
    return pl.pallas_call(
        _outproj_body,
        grid=(SEQ // OUT_TS,),
        in_specs=[
            pl.BlockSpec((OUT_TS, D_MODEL), lambda i: (i, 0)),
            pl.BlockSpec((OUT_TS, D_LRU + D_RET), lambda i: (i, 0)),
            pl.BlockSpec((D_MODEL, D_MODEL), lambda i: (0, 0)),
        ],
        out_specs=pl.BlockSpec((OUT_TS, D_MODEL), lambda i: (i, 0)),
        out_shape=jax.ShapeDtypeStruct((SEQ, D_MODEL), F32),
        compiler_params=_params("parallel"),
        name="outproj",
    )(x, y, w_out)


def _ffn_body(h_ref, ln2_ref, wg_ref, wu_ref, wd_ref, fnw_ref, o_ref, u_ref):
    f = pl.program_id(1)

    @pl.when(f == 0)
    def _():
        def step(c, carry):
            r0 = pl.multiple_of(c * NORM_ROWS, NORM_ROWS)
            hrows = h_ref[pl.ds(r0, NORM_ROWS), :]
            u_ref[pl.ds(r0, NORM_ROWS), :] = _rmsnorm_rows(hrows, ln2_ref[...]).astype(BF16)
            o_ref[pl.ds(r0, NORM_ROWS), :] = hrows
            return carry

        lax.fori_loop(0, FFN_TS // NORM_ROWS, step, 0)

    u = u_ref[...]
    gate = jnp.dot(u, wg_ref[...], preferred_element_type=F32)
    up = jnp.dot(u, wu_ref[...], preferred_element_type=F32)
    act = ((gate * _sigmoid(gate)) * up).astype(BF16)
    o_ref[...] += jnp.dot(act, wd_ref[...], preferred_element_type=F32)

    @pl.when(f == pl.num_programs(1) - 1)
    def _():
        def step(c, carry):
            r0 = pl.multiple_of(c * NORM_ROWS, NORM_ROWS)
            rows = pl.ds(r0, NORM_ROWS)
            o_ref[rows, :] = _rmsnorm_rows(o_ref[rows, :], fnw_ref[...])
            return carry

        lax.fori_loop(0, FFN_TS // NORM_ROWS, step, 0)


def _ffn(h1, ln2, wg, wu, wd, fnw):
    return pl.pallas_call(
        _ffn_body,
        grid=(SEQ // FFN_TS, D_FF // FFN_TF),
        in_specs=[
            pl.BlockSpec((FFN_TS, D_MODEL), lambda i, f: (i, 0)),
            pl.BlockSpec((1, D_MODEL), lambda i, f: (0, 0)),
            pl.BlockSpec((None, D_MODEL, FFN_TF), lambda i, f: (f, 0, 0)),
            pl.BlockSpec((None, D_MODEL, FFN_TF), lambda i, f: (f, 0, 0)),
            pl.BlockSpec((FFN_TF, D_MODEL), lambda i, f: (f, 0)),
            pl.BlockSpec((1, D_MODEL), lambda i, f: (0, 0)),
        ],
        out_specs=pl.BlockSpec((FFN_TS, D_MODEL), lambda i, f: (i, 0)),
        out_shape=jax.ShapeDtypeStruct((SEQ, D_MODEL), F32),
        scratch_shapes=[pltpu.VMEM((FFN_TS, D_MODEL), BF16)],
        compiler_params=_params("parallel", "arbitrary"),
        name="ffn",
    )(h1, ln2, wg, wu, wd, fnw)


def _retention_tables():
    H, Dh, C = RET_HEADS, RET_HD, RET_C
    inv_freq = ROPE_BASE ** (-np.arange(0, Dh, 2, dtype=np.float64) / Dh)
    inv2 = np.concatenate([inv_freq, inv_freq])
    off = np.arange(MIX_TS, dtype=np.float64)[:, None] * inv2[None, :]
    base = (np.arange(N_BLK, dtype=np.float64) * MIX_TS)[:, None, None] * inv2[None, None, :]
    log_gamma = np.log1p(-np.exp2(-5.0 - np.arange(H, dtype=np.float64)))
    idx = np.arange(C)
    diff = idx[:, None] - idx[None, :]
    decay = np.where(diff >= 0, np.exp(log_gamma[:, None, None] * np.maximum(diff, 0)[None]), 0.0)
    zeta = np.exp(log_gamma[:, None] * (C - 1 - idx)[None, :])
    xi = np.exp(log_gamma[:, None] * (idx + 1)[None, :])
    gc = np.exp(log_gamma * C)
    per_head = lambda t: np.broadcast_to(t[:, :, None], (H, C, Dh))
    tables = (np.stack([np.cos(off), np.sin(off)]),
              np.concatenate([np.cos(base), np.sin(base)], axis=1),
              decay,
              np.stack([per_head(xi), per_head(zeta)]),
              np.broadcast_to(gc[:, None, None], (H, 1, Dh)))
    return tuple(jnp.asarray(np.ascontiguousarray(t, dtype=np.float32)) for t in tables)


def kernel(x, ln1_w, w_in, conv_w, conv_b, gate_a_w, gate_a_b, gate_x_w, gate_x_b, lru_lambda, ret_gn_w,
           w_out, ln2_w, w_ffn_gate, w_ffn_up, w_ffn_down, final_norm_w):
    x2 = x.reshape(SEQ, D_MODEL)
    row = lambda v: v.reshape(1, -1)
    by_lb = lambda v: v.reshape(-1, 1, LANES)
    wcat = jnp.concatenate([gate_a_w[0], gate_x_w[0]], axis=-1).astype(BF16)
    cw = conv_w[0].reshape(CONV_W, N_LRU_LB, LANES).transpose(1, 0, 2)
    rot, base, decay, xz, gc = _retention_tables()
    unit_tab = jnp.concatenate(
        [cw, by_lb(conv_b[0]), by_lb(gate_a_b[0]), by_lb(gate_x_b[0]), by_lb(lru_lambda[0]), gc,
         by_lb(ret_gn_w[0]), jnp.zeros((N_LRU_LB, UNIT_ROWS - U_GN_W - 1, LANES), F32)], axis=1)
    w_slabs = (w_in[0].reshape(D_MODEL, MIX_LB, MIX_J, LANES).transpose(2, 0, 1, 3)
               .reshape(MIX_J, D_MODEL, MIX_NW).astype(BF16))
    y, wg, wu, wd, wo = _mix(x2, row(ln1_w[0]), w_slabs, unit_tab, wcat, rot, base, decay, xz,
                             w_ffn_gate[0], w_ffn_up[0], w_ffn_down[0], w_out[0])
    h1 = _outproj(x2, y, wo)
    out = _ffn(h1, row(ln2_w[0]), wg, wu, wd, row(final_norm_w))
    return out.reshape(1, SEQ, D_MODEL)
```

```python
import jax
import jax.numpy as jnp
import numpy as np
from jax import lax
from jax.experimental import pallas as pl
from jax.experimental.pallas import tpu as pltpu

D_MODEL = 2048
SEQ = 8192
D_LRU = 1024
D_RET = 1024
CONV_W = 4
RET_HEADS = 8
RET_HD = 128
ROPE_BASE = 10000.0
D_FF = 5632
D_IN = 6144
EPS = 1e-6
GELU_C = 0.7978845608028654

SUBLANES = 8
LANES = 128
VMEM_LIMIT_BYTES = 56 * 1024 * 1024

F32 = jnp.float32
BF16 = jnp.bfloat16

NORM_ROWS = 128
MIX_TS = 512
MIX_J = 8
MIX_NW = D_IN // MIX_J
MIX_LB = MIX_NW // LANES
N_BLK = SEQ // MIX_TS
N_LRU_LB = D_LRU // LANES
N_REST_LB = (D_IN - D_LRU) // LANES
LRU_R = 128
LRU_SUB = MIX_TS // LRU_R
RET_C = 256
OUT_TS = 512
FFN_TS, FFN_TF = 1024, 512
FFN_HALF = 512
U_CONV_B, U_GATE_A_B, U_GATE_X_B, U_LAMBDA, U_CHUNK_DECAY, U_GN_W = 4, 5, 6, 7, 8, 9
UNIT_ROWS = 16


def _rmsnorm_rows(x, w):
    ms = jnp.mean(x * x, axis=-1, keepdims=True)
    return (x * lax.rsqrt(ms + EPS)) * w


def _sigmoid(x):
    return 0.5 * jnp.tanh(0.5 * x) + 0.5


def _params(*sem):
    return pltpu.CompilerParams(dimension_semantics=sem, vmem_limit_bytes=VMEM_LIMIT_BYTES)


def _lru_sub_block(r0, hc, slot, m, rd_lru, rd_rest, ysc_ref, sa_ref, sb_ref, cw, cb, wcat, ba, bx, lsl4):
    nt = LRU_R // SUBLANES
    base = r0 + SUBLANES
    xc = cb + rd_lru[m, pl.ds(base - 3, LRU_R), :] * cw[0:1]
    xc = xc + rd_lru[m, pl.ds(base - 2, LRU_R), :] * cw[1:2]
    xc = xc + rd_lru[m, pl.ds(base - 1, LRU_R), :] * cw[2:3]
    xc = xc + rd_lru[m, pl.ds(base, LRU_R), :] * cw[3:4]
    gates = jnp.dot(xc.astype(BF16), wcat, preferred_element_type=F32)
    tr = jnp.tanh(0.5 * (gates[:, :LANES] + ba))
    ti = jnp.tanh(0.5 * (gates[:, LANES:] + bx))
    log_a = (tr + 1.0) * lsl4
    a = jnp.exp(log_a)
    v = -jnp.tanh(log_a)
    coef = jnp.where(v > 0.0, v * lax.rsqrt((v + v) * (1.0 + v)), 0.0)
    b = coef * ((ti + 1.0) * xc)
    a3 = a.reshape(nt, SUBLANES, LANES)
    b3 = b.reshape(nt, SUBLANES, LANES)
    for s in (1, 2, 4):
        sa_ref[slot, :, SUBLANES:2 * SUBLANES, :] = a3
        sb_ref[slot, :, SUBLANES:2 * SUBLANES, :] = b3
        a_sh = sa_ref[slot, :, SUBLANES - s:2 * SUBLANES - s, :]
        b_sh = sb_ref[slot, :, SUBLANES - s:2 * SUBLANES - s, :]
        b3 = a3 * b_sh + b3
        a3 = a3 * a_sh
    tiles = []
    for t in range(nt):
        ht = a3[t] * hc + b3[t]
        hc = ht[SUBLANES - 1:SUBLANES, :]
        tiles.append(ht)
    h = jnp.concatenate(tiles, axis=0)
    g = rd_rest[m, pl.ds(r0, LRU_R), :].astype(F32)
    tg = jnp.tanh(g * (GELU_C + (GELU_C * 0.044715) * (g * g)))
    hg = 0.5 * g
    ysc_ref[m, pl.ds(r0, LRU_R), :] = (h * (hg * tg + hg)).astype(BF16)
    return hc


def _mix_body(x_ref, ln_ref, w_ref, unit_ref, wg_ref, rot_ref, base_ref, dec_ref, xz_ref,
              fg_ref, fu_ref, fd_ref, fo_ref,
              y_ref, bgu_ref, bd_ref, bo_ref,
              u_ref, plru_a, plru_b, prest_a, prest_b, ysc_ref, sa_ref, sb_ref, h_ref, state_ref,
              cq_ref, sq_ref, ck_ref, sk_ref):
    i = pl.program_id(0)
    j = pl.program_id(1)
    m = j
    nt = LRU_R // SUBLANES

    for f in range(D_FF // FFN_TF):
        bgu_ref[f, :, 0:FFN_TF] = fg_ref[:, f * FFN_TF:(f + 1) * FFN_TF].astype(BF16)
        bgu_ref[f, :, FFN_TF:2 * FFN_TF] = fu_ref[:, f * FFN_TF:(f + 1) * FFN_TF].astype(BF16)
    bd_ref[...] = fd_ref[...].astype(BF16)
    bo_ref[...] = fo_ref[...].astype(BF16)

    @pl.when(jnp.logical_and(i == 0, j == 0))
    def _():
        h_ref[...] = jnp.zeros_like(h_ref)
        state_ref[...] = jnp.zeros_like(state_ref)
        plru_a[:, 0:SUBLANES, :] = jnp.zeros((N_LRU_LB + 1, SUBLANES, LANES), F32)

    @pl.when(j == 0)
    def _():
        def step(r, carry):
            r0 = pl.multiple_of(r * NORM_ROWS, NORM_ROWS)
            x = x_ref[pl.ds(r0, NORM_ROWS), :]
            u_ref[pl.ds(r0, NORM_ROWS), :] = _rmsnorm_rows(x, ln_ref[...]).astype(BF16)
            return carry

        lax.fori_loop(0, MIX_TS // NORM_ROWS, step, 0)

        blk = jnp.maximum(i - 1, 0)
        cb_, sb_ = base_ref[blk][0:1], base_ref[blk][1:2]
        cos = cb_ * rot_ref[0] - sb_ * rot_ref[1]
        sin = sb_ * rot_ref[0] + cb_ * rot_ref[1]
        lane = lax.broadcasted_iota(jnp.int32, (1, RET_HD), 1)
        sin = jnp.where(lane < RET_HD // 2, -sin, sin)
        cq_ref[...] = cos
        sq_ref[...] = sin
        ck_ref[...] = cos * (RET_HD ** -0.5)
        sk_ref[...] = sin * (RET_HD ** -0.5)

    def head_groups(rd_lru, rd_rest, wr_lru):
        sa_ref[:, :, 0:SUBLANES, :] = jnp.ones((LRU_SUB, nt, SUBLANES, LANES), F32)
        sb_ref[:, :, 0:SUBLANES, :] = jnp.zeros((LRU_SUB, nt, SUBLANES, LANES), F32)
        unit = unit_ref[m]
        cw, cb = unit[0:CONV_W], unit[U_CONV_B:U_CONV_B + 1]
        ba, bx = unit[U_GATE_A_B:U_GATE_A_B + 1], unit[U_GATE_X_B:U_GATE_X_B + 1]
        lam = unit[U_LAMBDA:U_LAMBDA + 1]
        gc, gnw = unit[U_CHUNK_DECAY:U_CHUNK_DECAY + 1], unit[U_GN_W:U_GN_W + 1]
        lsl4 = -4.0 * (jnp.maximum(-lam, 0.0) + jnp.log1p(jnp.exp(-jnp.abs(lam))))
        hc = h_ref[m][0:1, :]
        for slot in range(LRU_SUB):
            hc = _lru_sub_block(slot * LRU_R, hc, slot, m, rd_lru, rd_rest, ysc_ref, sa_ref, sb_ref,
                                cw, cb, wg_ref[m], ba, bx, lsl4)
        h_ref[m] = jnp.broadcast_to(hc, (SUBLANES, LANES))
        wr_lru[m, 0:SUBLANES, :] = rd_lru[m, pl.ds(MIX_TS, SUBLANES), :]

        state = state_ref[m]
        for c in range(MIX_TS // RET_C):
            rows = slice(c * RET_C, (c + 1) * RET_C)
            q = rd_rest[8 + m, rows, :].astype(F32)
            k = rd_rest[16 + m, rows, :].astype(F32)
            vb = rd_rest[24 + m, rows, :]
            g = rd_rest[32 + m, rows, :].astype(F32)
            qr = q * cq_ref[rows, :] + pltpu.roll(q, RET_HD // 2, 1) * sq_ref[rows, :]
            kr = k * ck_ref[rows, :] + pltpu.roll(k, RET_HD // 2, 1) * sk_ref[rows, :]
            scores = lax.dot_general(qr.astype(BF16), kr.astype(BF16), (((1,), (1,)), ((), ())),
                                     preferred_element_type=F32)
            p = (scores * dec_ref[m]).astype(BF16)
            lhs = jnp.concatenate([p, (qr * xz_ref[0, m]).astype(BF16)], axis=1)
            rhs = jnp.concatenate([vb, state.astype(BF16)], axis=0)
            o = jnp.dot(lhs, rhs, preferred_element_type=F32)
            kz = (kr * xz_ref[1, m]).astype(BF16)
            kv = lax.dot_general(kz, vb, (((0,), (0,)), ((), ())), preferred_element_type=F32)
            state = state * gc + kv
            mu = jnp.mean(o, axis=-1, keepdims=True)
            d = o - mu
            var = jnp.mean(d * d, axis=-1, keepdims=True)
            on = (d * lax.rsqrt(var + EPS)) * gnw
            ysc_ref[N_LRU_LB + m, rows, :] = ((g * _sigmoid(g)) * on).astype(BF16)
        state_ref[m] = state

    def projection(wr_lru, wr_rest):
        res = jnp.dot(u_ref[...], w_ref[...], preferred_element_type=F32)
        for kk in range(MIX_LB):
            lb = j * MIX_LB + kk
            blk = res[:, kk * LANES:(kk + 1) * LANES]
            wr_lru[jnp.minimum(lb, N_LRU_LB), pl.ds(SUBLANES, MIX_TS), :] = blk
            wr_rest[jnp.where(lb >= N_LRU_LB, lb - N_LRU_LB, N_REST_LB), :, :] = blk.astype(BF16)

    first, last, even = i == 0, i == N_BLK, lax.rem(i, 2) == 0
    mid = jnp.logical_not(jnp.logical_or(first, last))

    @pl.when(first)
    def _():
        projection(plru_a, prest_a)

    @pl.when(jnp.logical_and(mid, even))
    def _():
        head_groups(plru_b, prest_b, plru_a)
        projection(plru_a, prest_a)

    @pl.when(jnp.logical_and(mid, jnp.logical_not(even)))
    def _():
        head_groups(plru_a, prest_a, plru_b)
        projection(plru_b, prest_b)

    @pl.when(last)
    def _():
        head_groups(plru_b, prest_b, plru_a)

    @pl.when(jnp.logical_and(j == MIX_J - 1, jnp.logical_not(first)))
    def _():
        for lb in range(2 * N_LRU_LB):
            y_ref[:, lb * LANES:(lb + 1) * LANES] = ysc_ref[lb]


def _mix(x, ln, w_slabs, unit_tab, wcat, rot, base, decay, xz, w_gate, w_up, w_down, w_out):
    assert N_BLK % 2 == 0
    prev_blk = lambda i, j: (jnp.maximum(i - 1, 0), 0)
    const = lambda a: pl.BlockSpec(a.shape, lambda i, j: (0,) * a.ndim)
    chunk = lambda i, j, n: jnp.minimum(i * MIX_J + j, n - 1)
    nt = LRU_R // SUBLANES
    n_steps = N_BLK * MIX_J
    up_rows = D_MODEL // n_steps
    down_rows = 4 * up_rows
    n_up, n_down, n_ff = D_MODEL // up_rows, D_FF // down_rows, D_FF // FFN_TF
    up_in = pl.BlockSpec((up_rows, D_FF), lambda i, j: (chunk(i, j, n_up), 0))
    up_out = pl.BlockSpec((n_ff, up_rows, 2 * FFN_TF), lambda i, j: (0, chunk(i, j, n_up), 0))
    down_io = pl.BlockSpec((down_rows, D_MODEL), lambda i, j: (chunk(i, j, n_down), 0))
    out_io = pl.BlockSpec((up_rows, D_MODEL), lambda i, j: (chunk(i, j, n_up), 0))
    lru_buf = pltpu.VMEM((N_LRU_LB + 1, SUBLANES + MIX_TS, LANES), F32)
    rest_buf = pltpu.VMEM((N_REST_LB + 1, MIX_TS, LANES), BF16)
    return pl.pallas_call(
        _mix_body,
        grid=(N_BLK + 1, MIX_J),
        in_specs=[
            pl.BlockSpec((MIX_TS, D_MODEL), lambda i, j: (jnp.minimum(i, N_BLK - 1), 0)),
            pl.BlockSpec((1, D_MODEL), lambda i, j: (0, 0)),
            pl.BlockSpec((D_MODEL, MIX_NW), lambda i, j: (0, j)),
            const(unit_tab), const(wcat), const(rot), const(base), const(decay), const(xz),
            up_in, up_in, down_io, out_io,
        ],
        out_specs=[pl.BlockSpec((MIX_TS, D_LRU + D_RET), prev_blk), up_out, down_io, out_io],
        out_shape=[
            jax.ShapeDtypeStruct((SEQ, D_LRU + D_RET), BF16),
            jax.ShapeDtypeStruct((n_ff, D_MODEL, 2 * FFN_TF), BF16),
            jax.ShapeDtypeStruct(w_down.shape, BF16),
            jax.ShapeDtypeStruct(w_out.shape, BF16),
        ],
        scratch_shapes=[
            pltpu.VMEM((MIX_TS, D_MODEL), BF16),
            lru_buf, lru_buf, rest_buf, rest_buf,
            pltpu.VMEM((2 * N_LRU_LB, MIX_TS, LANES), BF16),
            pltpu.VMEM((LRU_SUB, nt, 2 * SUBLANES, LANES), F32),
            pltpu.VMEM((LRU_SUB, nt, 2 * SUBLANES, LANES), F32),
            pltpu.VMEM((N_LRU_LB, SUBLANES, LANES), F32),
            pltpu.VMEM((RET_HEADS, RET_HD, RET_HD), F32),
            pltpu.VMEM((MIX_TS, RET_HD), F32),
            pltpu.VMEM((MIX_TS, RET_HD), F32),
            pltpu.VMEM((MIX_TS, RET_HD), F32),
            pltpu.VMEM((MIX_TS, RET_HD), F32),
        ],
        compiler_params=_params("arbitrary", "arbitrary"),
        name="mix",
    )(x, ln, w_slabs, unit_tab, wcat, rot, base, decay, xz, w_gate, w_up, w_down, w_out)


def _outproj_body(x_ref, y_ref, wo_ref, o_ref):
    o_ref[...] = x_ref[...] + jnp.dot(y_ref[...], wo_ref[...], preferred_element_type=F32)


def _outproj(x, y, w_out):
    return pl.pallas_call(
        _outproj_body,
        grid=(SEQ // OUT_TS,),
        in_specs=[
            pl.BlockSpec((OUT_TS, D_MODEL), lambda i: (i, 0)),
            pl.BlockSpec((OUT_TS, D_LRU + D_RET), lambda i: (i, 0)),
            pl.BlockSpec((D_MODEL, D_MODEL), lambda i: (0, 0)),
        ],
        out_specs=pl.BlockSpec((OUT_TS, D_MODEL), lambda i: (i, 0)),
        out_shape=jax.ShapeDtypeStruct((SEQ, D_MODEL), F32),
        compiler_params=_params("parallel"),
        name="outproj",
    )(x, y, w_out)


def _ffn_body(h_hbm, ln2_ref, wgu_ref, wd_ref, fnw_ref, o_ref, u_ref, hbuf_ref, hsem):
    i = pl.program_id(0)
    f = pl.program_id(1)

    def h_copy(blk):
        return pltpu.make_async_copy(h_hbm.at[pl.ds(blk * FFN_TS, FFN_TS), :], hbuf_ref, hsem)

    @pl.when(jnp.logical_and(i == 0, f == 0))
    def _():
        h_copy(0).start()

    @pl.when(f == 0)
    def _():
        h_copy(i).wait()

        def step(c, carry):
            r0 = pl.multiple_of(c * NORM_ROWS, NORM_ROWS)
            hrows = hbuf_ref[pl.ds(r0, NORM_ROWS), :]
            u_ref[pl.ds(r0, NORM_ROWS), :] = _rmsnorm_rows(hrows, ln2_ref[...]).astype(BF16)
            o_ref[pl.ds(r0, NORM_ROWS), :] = hrows
            return carry

        lax.fori_loop(0, FFN_TS // NORM_ROWS, step, 0)

    @pl.when(jnp.logical_and(f == 1, i + 1 < pl.num_programs(0)))
    def _():
        h_copy(i + 1).start()

    for half in range(FFN_TS // FFN_HALF):
        rows = slice(half * FFN_HALF, (half + 1) * FFN_HALF)
        gu = jnp.dot(u_ref[rows, :], wgu_ref[...], preferred_element_type=F32)
        gate, up = gu[:, :FFN_TF], gu[:, FFN_TF:]
        act = ((gate * _sigmoid(gate)) * up).astype(BF16)
        o_ref[rows, :] += jnp.dot(act, wd_ref[...], preferred_element_type=F32)

    @pl.when(f == pl.num_programs(1) - 1)
    def _():
        def step(c, carry):
            r0 = pl.multiple_of(c * NORM_ROWS, NORM_ROWS)
            rows = pl.ds(r0, NORM_ROWS)
            o_ref[rows, :] = _rmsnorm_rows(o_ref[rows, :], fnw_ref[...])
            return carry

        lax.fori_loop(0, FFN_TS // NORM_ROWS, step, 0)


def _ffn(h1, ln2, wgu, wd, fnw):
    return pl.pallas_call(
        _ffn_body,
        grid=(SEQ // FFN_TS, D_FF // FFN_TF),
        in_specs=[
            pl.BlockSpec(memory_space=pl.ANY),
            pl.BlockSpec((1, D_MODEL), lambda i, f: (0, 0)),
            pl.BlockSpec((None, D_MODEL, 2 * FFN_TF), lambda i, f: (f, 0, 0)),
            pl.BlockSpec((FFN_TF, D_MODEL), lambda i, f: (f, 0)),
            pl.BlockSpec((1, D_MODEL), lambda i, f: (0, 0)),
        ],
        out_specs=pl.BlockSpec((FFN_TS, D_MODEL), lambda i, f: (i, 0)),
        out_shape=jax.ShapeDtypeStruct((SEQ, D_MODEL), F32),
        scratch_shapes=[
            pltpu.VMEM((FFN_TS, D_MODEL), BF16),
            pltpu.VMEM((FFN_TS, D_MODEL), F32),
            pltpu.SemaphoreType.DMA(()),
        ],
        compiler_params=_params("arbitrary", "arbitrary"),
        name="ffn",
    )(h1, ln2, wgu, wd, fnw)


def _retention_tables():
    H, Dh, C = RET_HEADS, RET_HD, RET_C
    inv_freq = ROPE_BASE ** (-np.arange(0, Dh, 2, dtype=np.float64) / Dh)
    inv2 = np.concatenate([inv_freq, inv_freq])
    off = np.arange(MIX_TS, dtype=np.float64)[:, None] * inv2[None, :]
    base = (np.arange(N_BLK, dtype=np.float64) * MIX_TS)[:, None, None] * inv2[None, None, :]
    log_gamma = np.log1p(-np.exp2(-5.0 - np.arange(H, dtype=np.float64)))
    idx = np.arange(C)
    diff = idx[:, None] - idx[None, :]
    decay = np.where(diff >= 0, np.exp(log_gamma[:, None, None] * np.maximum(diff, 0)[None]), 0.0)
    zeta = np.exp(log_gamma[:, None] * (C - 1 - idx)[None, :])
    xi = np.exp(log_gamma[:, None] * (idx + 1)[None, :])
    gc = np.exp(log_gamma * C)
    per_head = lambda t: np.broadcast_to(t[:, :, None], (H, C, Dh))
    tables = (np.stack([np.cos(off), np.sin(off)]),
              np.concatenate([np.cos(base), np.sin(base)], axis=1),
              decay,
              np.stack([per_head(xi), per_head(zeta)]),
              np.broadcast_to(gc[:, None, None], (H, 1, Dh)))
    return tuple(jnp.asarray(np.ascontiguousarray(t, dtype=np.float32)) for t in tables)


def kernel(x, ln1_w, w_in, conv_w, conv_b, gate_a_w, gate_a_b, gate_x_w, gate_x_b, lru_lambda, ret_gn_w,
           w_out, ln2_w, w_ffn_gate, w_ffn_up, w_ffn_down, final_norm_w):
    x2 = x.reshape(SEQ, D_MODEL)
    row = lambda v: v.reshape(1, -1)
    by_lb = lambda v: v.reshape(-1, 1, LANES)
    wcat = jnp.concatenate([gate_a_w[0], gate_x_w[0]], axis=-1).astype(BF16)
    cw = conv_w[0].reshape(CONV_W, N_LRU_LB, LANES).transpose(1, 0, 2)
    rot, base, decay, xz, gc = _retention_tables()
    unit_tab = jnp.concatenate(
        [cw, by_lb(conv_b[0]), by_lb(gate_a_b[0]), by_lb(gate_x_b[0]), by_lb(lru_lambda[0]), gc,
         by_lb(ret_gn_w[0]), jnp.zeros((N_LRU_LB, UNIT_ROWS - U_GN_W - 1, LANES), F32)], axis=1)
    y, wgu, wd, wo = _mix(x2, row(ln1_w[0]), w_in[0].astype(BF16), unit_tab, wcat, rot, base, decay, xz,
                          w_ffn_gate[0], w_ffn_up[0], w_ffn_down[0], w_out[0])
    h1 = _outproj(x2, y, wo)
    out = _ffn(h1, row(ln2_w[0]), wgu, wd, row(final_norm_w))
    return out.reshape(1, SEQ, D_MODEL)
```

```python
import jax
import jax.numpy as jnp
import numpy as np
from jax import lax
from jax.experimental import pallas as pl
from jax.experimental.pallas import tpu as pltpu

D_MODEL = 2048
SEQ = 8192
D_LRU = 1024
D_RET = 1024
CONV_W = 4
RET_HEADS = 8
RET_HD = 128
ROPE_BASE = 10000.0
D_FF = 5632
D_IN = 6144
EPS = 1e-6
GELU_C = 0.7978845608028654

SUBLANES = 8
LANES = 128
VMEM_LIMIT_BYTES = 56 * 1024 * 1024

F32 = jnp.float32
BF16 = jnp.bfloat16

NORM_ROWS = 128
MIX_TS = 512
MIX_J = 4
MIX_NW = D_IN // MIX_J
UNITS = 8 // MIX_J
MIX_LB = MIX_NW // LANES
N_BLK = SEQ // MIX_TS
N_LRU_LB = D_LRU // LANES
N_REST_LB = (D_IN - D_LRU) // LANES
LRU_R = 128
LRU_SUB = MIX_TS // LRU_R
RET_C = 256
OUT_TS = 512
FFN_TS, FFN_TF = 1024, 512
FFN_HALF = 512
U_CONV_B, U_GATE_A_B, U_GATE_X_B, U_LAMBDA, U_CHUNK_DECAY, U_GN_W = 4, 5, 6, 7, 8, 9
UNIT_ROWS = 16


def _rmsnorm_rows(x, w):
    ms = jnp.mean(x * x, axis=-1, keepdims=True)
    return (x * lax.rsqrt(ms + EPS)) * w


def _sigmoid(x):
    return 0.5 * jnp.tanh(0.5 * x) + 0.5


def _params(*sem):
    return pltpu.CompilerParams(dimension_semantics=sem, vmem_limit_bytes=VMEM_LIMIT_BYTES)


def _lru_sub_block(r0, hc, slot, m, rd_lru, rd_rest, ysc_ref, sa_ref, sb_ref, cw, cb, wcat, ba, bx, lsl4):
    nt = LRU_R // SUBLANES
    base = r0 + SUBLANES
    xc = cb + rd_lru[m, pl.ds(base - 3, LRU_R), :] * cw[0:1]
    xc = xc + rd_lru[m, pl.ds(base - 2, LRU_R), :] * cw[1:2]
    xc = xc + rd_lru[m, pl.ds(base - 1, LRU_R), :] * cw[2:3]
    xc = xc + rd_lru[m, pl.ds(base, LRU_R), :] * cw[3:4]
    gates = jnp.dot(xc.astype(BF16), wcat, preferred_element_type=F32)
    tr = jnp.tanh(0.5 * (gates[:, :LANES] + ba))
    ti = jnp.tanh(0.5 * (gates[:, LANES:] + bx))
    log_a = (tr + 1.0) * lsl4
    a = jnp.exp(log_a)
    v = -jnp.tanh(log_a)
    coef = jnp.where(v > 0.0, v * lax.rsqrt((v + v) * (1.0 + v)), 0.0)
    b = coef * ((ti + 1.0) * xc)
    a3 = a.reshape(nt, SUBLANES, LANES)
    b3 = b.reshape(nt, SUBLANES, LANES)
    for s in (1, 2, 4):
        sa_ref[slot, :, SUBLANES:2 * SUBLANES, :] = a3
        sb_ref[slot, :, SUBLANES:2 * SUBLANES, :] = b3
        a_sh = sa_ref[slot, :, SUBLANES - s:2 * SUBLANES - s, :]
        b_sh = sb_ref[slot, :, SUBLANES - s:2 * SUBLANES - s, :]
        b3 = a3 * b_sh + b3
        a3 = a3 * a_sh
    tiles = []
    for t in range(nt):
        ht = a3[t] * hc + b3[t]
        hc = ht[SUBLANES - 1:SUBLANES, :]
        tiles.append(ht)
    h = jnp.concatenate(tiles, axis=0)
    g = rd_rest[m, pl.ds(r0, LRU_R), :].astype(F32)
    tg = jnp.tanh(g * (GELU_C + (GELU_C * 0.044715) * (g * g)))
    hg = 0.5 * g
    ysc_ref[m, pl.ds(r0, LRU_R), :] = (h * (hg * tg + hg)).astype(BF16)
    return hc


def _mix_body(x_hbm, ln_ref, w_ref, unit_ref, wg_ref, rot_ref, base_ref, dec_ref, xz_ref,
              fg_ref, fu_ref, fd_ref, fo_ref,
              ysc_ref, bgu_ref, bd_ref, bo_ref,
              u_ref, plru_a, plru_b, prest_a, prest_b, sa_ref, sb_ref, h_ref, state_ref,
              cq_ref, sq_ref, ck_ref, sk_ref, xbuf_ref, xsem):
    i = pl.program_id(0)
    j = pl.program_id(1)
    nt = LRU_R // SUBLANES

    for f in range(D_FF // FFN_TF):
        bgu_ref[f, :, 0:FFN_TF] = fg_ref[:, f * FFN_TF:(f + 1) * FFN_TF].astype(BF16)
        bgu_ref[f, :, FFN_TF:2 * FFN_TF] = fu_ref[:, f * FFN_TF:(f + 1) * FFN_TF].astype(BF16)
    bd_ref[...] = fd_ref[...].astype(BF16)
    bo_ref[...] = fo_ref[...].astype(BF16)

    @pl.when(jnp.logical_and(i == 0, j == 0))
    def _():
        h_ref[...] = jnp.zeros_like(h_ref)
        state_ref[...] = jnp.zeros_like(state_ref)
        plru_a[:, 0:SUBLANES, :] = jnp.zeros((N_LRU_LB + 1, SUBLANES, LANES), F32)

    def x_copy(blk):
        return pltpu.make_async_copy(x_hbm.at[pl.ds(blk * MIX_TS, MIX_TS), :], xbuf_ref, xsem)

    @pl.when(jnp.logical_and(i == 0, j == 0))
    def _():
        x_copy(0).start()

    @pl.when(jnp.logical_and(j == 0, i < N_BLK))
    def _():
        x_copy(i).wait()

        def step(r, carry):
            r0 = pl.multiple_of(r * NORM_ROWS, NORM_ROWS)
            x = xbuf_ref[pl.ds(r0, NORM_ROWS), :]
            u_ref[pl.ds(r0, NORM_ROWS), :] = _rmsnorm_rows(x, ln_ref[...]).astype(BF16)
            return carry

        lax.fori_loop(0, MIX_TS // NORM_ROWS, step, 0)

    @pl.when(jnp.logical_and(j == 1, i + 1 < N_BLK))
    def _():
        x_copy(i + 1).start()

    @pl.when(j == 0)
    def _():
        blk = jnp.maximum(i - 1, 0)
        cb_, sb_ = base_ref[blk][0:1], base_ref[blk][1:2]
        cos = cb_ * rot_ref[0] - sb_ * rot_ref[1]
        sin = sb_ * rot_ref[0] + cb_ * rot_ref[1]
        lane = lax.broadcasted_iota(jnp.int32, (1, RET_HD), 1)
        sin = jnp.where(lane < RET_HD // 2, -sin, sin)
        cq_ref[...] = cos
        sq_ref[...] = sin
        ck_ref[...] = cos * (RET_HD ** -0.5)
        sk_ref[...] = sin * (RET_HD ** -0.5)

    def head_groups(rd_lru, rd_rest, wr_lru):
        sa_ref[:, :, 0:SUBLANES, :] = jnp.ones((UNITS * LRU_SUB, nt, SUBLANES, LANES), F32)
        sb_ref[:, :, 0:SUBLANES, :] = jnp.zeros((UNITS * LRU_SUB, nt, SUBLANES, LANES), F32)
        for uu in range(UNITS):
            head_group(j * UNITS + uu, uu * LRU_SUB, rd_lru, rd_rest, wr_lru)

    def head_group(m, slot0, rd_lru, rd_rest, wr_lru):
        unit = unit_ref[m]
        cw, cb = unit[0:CONV_W], unit[U_CONV_B:U_CONV_B + 1]
        ba, bx = unit[U_GATE_A_B:U_GATE_A_B + 1], unit[U_GATE_X_B:U_GATE_X_B + 1]
        lam = unit[U_LAMBDA:U_LAMBDA + 1]
        gc, gnw = unit[U_CHUNK_DECAY:U_CHUNK_DECAY + 1], unit[U_GN_W:U_GN_W + 1]
        lsl4 = -4.0 * (jnp.maximum(-lam, 0.0) + jnp.log1p(jnp.exp(-jnp.abs(lam))))
        hc = h_ref[m][0:1, :]
        for slot in range(LRU_SUB):
            hc = _lru_sub_block(slot * LRU_R, hc, slot0 + slot, m, rd_lru, rd_rest, ysc_ref, sa_ref, sb_ref,
                                cw, cb, wg_ref[m], ba, bx, lsl4)
        h_ref[m] = jnp.broadcast_to(hc, (SUBLANES, LANES))
        wr_lru[m, 0:SUBLANES, :] = rd_lru[m, pl.ds(MIX_TS, SUBLANES), :]

        state = state_ref[m]
        for c in range(MIX_TS // RET_C):
            rows = slice(c * RET_C, (c + 1) * RET_C)
            q = rd_rest[8 + m, rows, :].astype(F32)
            k = rd_rest[16 + m, rows, :].astype(F32)
            vb = rd_rest[24 + m, rows, :]
            g = rd_rest[32 + m, rows, :].astype(F32)
            qr = q * cq_ref[rows, :] + pltpu.roll(q, RET_HD // 2, 1) * sq_ref[rows, :]
            kr = k * ck_ref[rows, :] + pltpu.roll(k, RET_HD // 2, 1) * sk_ref[rows, :]
            scores = lax.dot_general(qr.astype(BF16), kr.astype(BF16), (((1,), (1,)), ((), ())),
                                     preferred_element_type=F32)
            p = (scores * dec_ref[m]).astype(BF16)
            lhs = jnp.concatenate([p, (qr * xz_ref[0, m]).astype(BF16)], axis=1)
            rhs = jnp.concatenate([vb, state.astype(BF16)], axis=0)
            o = jnp.dot(lhs, rhs, preferred_element_type=F32)
            kz = (kr * xz_ref[1, m]).astype(BF16)
            kv = lax.dot_general(kz, vb, (((0,), (0,)), ((), ())), preferred_element_type=F32)
            state = state * gc + kv
            mu = jnp.mean(o, axis=-1, keepdims=True)
            d = o - mu
            var = jnp.mean(d * d, axis=-1, keepdims=True)
            on = (d * lax.rsqrt(var + EPS)) * gnw
            ysc_ref[N_LRU_LB + m, rows, :] = ((g * _sigmoid(g)) * on).astype(BF16)
        state_ref[m] = state

    def projection(wr_lru, wr_rest):
        res = jnp.dot(u_ref[...], w_ref[...], preferred_element_type=F32)
        for kk in range(MIX_LB):
            lb = j * MIX_LB + kk
            blk = res[:, kk * LANES:(kk + 1) * LANES]
            wr_lru[jnp.minimum(lb, N_LRU_LB), pl.ds(SUBLANES, MIX_TS), :] = blk
            wr_rest[jnp.where(lb >= N_LRU_LB, lb - N_LRU_LB, N_REST_LB), :, :] = blk.astype(BF16)

    first, last, even = i == 0, i == N_BLK, lax.rem(i, 2) == 0
    mid = jnp.logical_not(jnp.logical_or(first, last))

    @pl.when(first)
    def _():
        projection(plru_a, prest_a)

    @pl.when(jnp.logical_and(mid, even))
    def _():
        head_groups(plru_b, prest_b, plru_a)
        projection(plru_a, prest_a)

    @pl.when(jnp.logical_and(mid, jnp.logical_not(even)))
    def _():
        head_groups(plru_a, prest_a, plru_b)
        projection(plru_b, prest_b)

    @pl.when(last)
    def _():
        head_groups(plru_b, prest_b, plru_a)


def _mix(x, ln, w_slabs, unit_tab, wcat, rot, base, decay, xz, w_gate, w_up, w_down, w_out):
    assert N_BLK % 2 == 0
    prev_blk = lambda i, j: (jnp.maximum(i - 1, 0), 0, 0, 0)
    const = lambda a: pl.BlockSpec(a.shape, lambda i, j: (0,) * a.ndim, pipeline_mode=pl.Buffered(1))
    chunk = lambda i, j, n: jnp.minimum(i * MIX_J + j, n - 1)
    nt = LRU_R // SUBLANES
    n_steps = N_BLK * MIX_J
    up_rows = D_MODEL // n_steps
    down_rows = 4 * up_rows
    n_up, n_down, n_ff = D_MODEL // up_rows, D_FF // down_rows, D_FF // FFN_TF
    up_in = pl.BlockSpec((up_rows, D_FF), lambda i, j: (chunk(i, j, n_up), 0))
    up_out = pl.BlockSpec((n_ff, up_rows, 2 * FFN_TF), lambda i, j: (0, chunk(i, j, n_up), 0))
    down_io = pl.BlockSpec((down_rows, D_MODEL), lambda i, j: (chunk(i, j, n_down), 0))
    out_io = pl.BlockSpec((up_rows, D_MODEL), lambda i, j: (chunk(i, j, n_up), 0))
    lru_buf = pltpu.VMEM((N_LRU_LB + 1, SUBLANES + MIX_TS, LANES), F32)
    rest_buf = pltpu.VMEM((N_REST_LB + 1, MIX_TS, LANES), BF16)
    return pl.pallas_call(
        _mix_body,
        grid=(N_BLK + 1, MIX_J),
        in_specs=[
            pl.BlockSpec(memory_space=pl.ANY),
            pl.BlockSpec((1, D_MODEL), lambda i, j: (0, 0)),
            pl.BlockSpec((D_MODEL, MIX_NW), lambda i, j: (0, j)),
            const(unit_tab), const(wcat), const(rot), const(base), const(decay), const(xz),
            up_in, up_in, down_io, out_io,
        ],
        out_specs=[pl.BlockSpec((None, 2 * N_LRU_LB, MIX_TS, LANES), prev_blk), up_out, down_io, out_io],
        out_shape=[
            jax.ShapeDtypeStruct((N_BLK, 2 * N_LRU_LB, MIX_TS, LANES), BF16),
            jax.ShapeDtypeStruct((n_ff, D_MODEL, 2 * FFN_TF), BF16),
            jax.ShapeDtypeStruct(w_down.shape, BF16),
            jax.ShapeDtypeStruct(w_out.shape, BF16),
        ],
        scratch_shapes=[
            pltpu.VMEM((MIX_TS, D_MODEL), BF16),
            lru_buf, lru_buf, rest_buf, rest_buf,
            pltpu.VMEM((UNITS * LRU_SUB, nt, 2 * SUBLANES, LANES), F32),
            pltpu.VMEM((UNITS * LRU_SUB, nt, 2 * SUBLANES, LANES), F32),
            pltpu.VMEM((N_LRU_LB, SUBLANES, LANES), F32),
            pltpu.VMEM((RET_HEADS, RET_HD, RET_HD), F32),
            pltpu.VMEM((MIX_TS, RET_HD), F32),
            pltpu.VMEM((MIX_TS, RET_HD), F32),
            pltpu.VMEM((MIX_TS, RET_HD), F32),
            pltpu.VMEM((MIX_TS, RET_HD), F32),
            pltpu.VMEM((MIX_TS, D_MODEL), F32),
            pltpu.SemaphoreType.DMA(()),
        ],
        compiler_params=_params("arbitrary", "arbitrary"),
        name="mix",
    )(x, ln, w_slabs, unit_tab, wcat, rot, base, decay, xz, w_gate, w_up, w_down, w_out)


def _outproj_body(x_ref, y_ref, wo_ref, o_ref):
    y = jnp.concatenate([y_ref[lb] for lb in range(2 * N_LRU_LB)], axis=1)
    o_ref[...] = x_ref[...] + jnp.dot(y, wo_ref[...], preferred_element_type=F32)


def _outproj(x, y, w_out):
    return pl.pallas_call(
        _outproj_body,
        grid=(SEQ // OUT_TS,),
        in_specs=[
            pl.BlockSpec((OUT_TS, D_MODEL), lambda i: (i, 0)),
            pl.BlockSpec((None, 2 * N_LRU_LB, OUT_TS, LANES), lambda i: (i, 0, 0, 0)),
            pl.BlockSpec((D_MODEL, D_MODEL), lambda i: (0, 0)),
        ],
        out_specs=pl.BlockSpec((OUT_TS, D_MODEL), lambda i: (i, 0)),
        out_shape=jax.ShapeDtypeStruct((SEQ, D_MODEL), F32),
        compiler_params=_params("parallel"),
        name="outproj",
    )(x, y, w_out)


def _ffn_body(h_hbm, ln2_ref, wgu_ref, wd_ref, fnw_ref, o_ref, u_ref, hbuf_ref, hsem):
    i = pl.program_id(0)
    f = pl.program_id(1)

    def h_copy(blk):
        return pltpu.make_async_copy(h_hbm.at[pl.ds(blk * FFN_TS, FFN_TS), :], hbuf_ref, hsem)

    @pl.when(jnp.logical_and(i == 0, f == 0))
    def _():
        h_copy(0).start()

    @pl.when(f == 0)
    def _():
        h_copy(i).wait()

        def step(c, carry):
            r0 = pl.multiple_of(c * NORM_ROWS, NORM_ROWS)
            hrows = hbuf_ref[pl.ds(r0, NORM_ROWS), :]
            u_ref[pl.ds(r0, NORM_ROWS), :] = _rmsnorm_rows(hrows, ln2_ref[...]).astype(BF16)
            o_ref[pl.ds(r0, NORM_ROWS), :] = hrows
            return carry

        lax.fori_loop(0, FFN_TS // NORM_ROWS, step, 0)

    @pl.when(jnp.logical_and(f == 1, i + 1 < pl.num_programs(0)))
    def _():
        h_copy(i + 1).start()

    for half in range(FFN_TS // FFN_HALF):
        rows = slice(half * FFN_HALF, (half + 1) * FFN_HALF)
        gu = jnp.dot(u_ref[rows, :], wgu_ref[...], preferred_element_type=F32)
        gate, up = gu[:, :FFN_TF], gu[:, FFN_TF:]
        act = ((gate * _sigmoid(gate)) * up).astype(BF16)
        o_ref[rows, :] += jnp.dot(act, wd_ref[...], preferred_element_type=F32)

    @pl.when(f == pl.num_programs(1) - 1)
    def _():
        def step(c, carry):
            r0 = pl.multiple_of(c * NORM_ROWS, NORM_ROWS)
            rows = pl.ds(r0, NORM_ROWS)
            o_ref[rows, :] = _rmsnorm_rows(o_ref[rows, :], fnw_ref[...])
            return carry

        lax.fori_loop(0, FFN_TS // NORM_ROWS, step, 0)


def _ffn(h1, ln2, wgu, wd, fnw):
    return pl.pallas_call(
        _ffn_body,
        grid=(SEQ // FFN_TS, D_FF // FFN_TF),
        in_specs=[
            pl.BlockSpec(memory_space=pl.ANY),
            pl.BlockSpec((1, D_MODEL), lambda i, f: (0, 0)),
            pl.BlockSpec((None, D_MODEL, 2 * FFN_TF), lambda i, f: (f, 0, 0)),
            pl.BlockSpec((FFN_TF, D_MODEL), lambda i, f: (f, 0)),
            pl.BlockSpec((1, D_MODEL), lambda i, f: (0, 0)),
        ],
        out_specs=pl.BlockSpec((FFN_TS, D_MODEL), lambda i, f: (i, 0)),
        out_shape=jax.ShapeDtypeStruct((SEQ, D_MODEL), F32),
        scratch_shapes=[
            pltpu.VMEM((FFN_TS, D_MODEL), BF16),
            pltpu.VMEM((FFN_TS, D_MODEL), F32),
            pltpu.SemaphoreType.DMA(()),
        ],
        compiler_params=_params("arbitrary", "arbitrary"),
        name="ffn",
    )(h1, ln2, wgu, wd, fnw)


def _retention_tables():
    H, Dh, C = RET_HEADS, RET_HD, RET_C
    inv_freq = ROPE_BASE ** (-np.arange(0, Dh, 2, dtype=np.float64) / Dh)
    inv2 = np.concatenate([inv_freq, inv_freq])
    off = np.arange(MIX_TS, dtype=np.float64)[:, None] * inv2[None, :]
    base = (np.arange(N_BLK, dtype=np.float64) * MIX_TS)[:, None, None] * inv2[None, None, :]
    log_gamma = np.log1p(-np.exp2(-5.0 - np.arange(H, dtype=np.float64)))
    idx = np.arange(C)
    diff = idx[:, None] - idx[None, :]
    decay = np.where(diff >= 0, np.exp(log_gamma[:, None, None] * np.maximum(diff, 0)[None]), 0.0)
    zeta = np.exp(log_gamma[:, None] * (C - 1 - idx)[None, :])
    xi = np.exp(log_gamma[:, None] * (idx + 1)[None, :])
    gc = np.exp(log_gamma * C)
    per_head = lambda t: np.broadcast_to(t[:, :, None], (H, C, Dh))
    tables = (np.stack([np.cos(off), np.sin(off)]),
              np.concatenate([np.cos(base), np.sin(base)], axis=1),
              decay,
              np.stack([per_head(xi), per_head(zeta)]),
              np.broadcast_to(gc[:, None, None], (H, 1, Dh)))
    return tuple(jnp.asarray(np.ascontiguousarray(t, dtype=np.float32)) for t in tables)


def kernel(x, ln1_w, w_in, conv_w, conv_b, gate_a_w, gate_a_b, gate_x_w, gate_x_b, lru_lambda, ret_gn_w,
           w_out, ln2_w, w_ffn_gate, w_ffn_up, w_ffn_down, final_norm_w):
    x2 = x.reshape(SEQ, D_MODEL)
    row = lambda v: v.reshape(1, -1)
    by_lb = lambda v: v.reshape(-1, 1, LANES)
    wcat = jnp.concatenate([gate_a_w[0], gate_x_w[0]], axis=-1).astype(BF16)
    cw = conv_w[0].reshape(CONV_W, N_LRU_LB, LANES).transpose(1, 0, 2)
    rot, base, decay, xz, gc = _retention_tables()
    unit_tab = jnp.concatenate(
        [cw, by_lb(conv_b[0]), by_lb(gate_a_b[0]), by_lb(gate_x_b[0]), by_lb(lru_lambda[0]), gc,
         by_lb(ret_gn_w[0]), jnp.zeros((N_LRU_LB, UNIT_ROWS - U_GN_W - 1, LANES), F32)], axis=1)
    y, wgu, wd, wo = _mix(x2, row(ln1_w[0]), w_in[0].astype(BF16), unit_tab, wcat, rot, base, decay, xz,
                          w_ffn_gate[0], w_ffn_up[0], w_ffn_down[0], w_out[0])
    h1 = _outproj(x2, y, wo)
    out = _ffn(h1, row(ln2_w[0]), wgu, wd, row(final_norm_w))
    return out.reshape(1, SEQ, D_MODEL)
```

```python
import jax
import jax.numpy as jnp
import numpy as np
from jax import lax
from jax.experimental import pallas as pl
from jax.experimental.pallas import tpu as pltpu

D_MODEL = 2048
SEQ = 8192
D_LRU = 1024
D_RET = 1024
CONV_W = 4
RET_HEADS = 8
RET_HD = 128
ROPE_BASE = 10000.0
D_FF = 5632
D_IN = 6144
EPS = 1e-6
GELU_C = 0.7978845608028654

SUBLANES = 8
LANES = 128
VMEM_LIMIT_BYTES = 56 * 1024 * 1024

F32 = jnp.float32
BF16 = jnp.bfloat16

NORM_ROWS = 128
MIX_TS = 512
MIX_J = 4
MIX_NW = D_IN // MIX_J
UNITS = 8 // MIX_J
MIX_LB = MIX_NW // LANES
N_BLK = SEQ // MIX_TS
N_LRU_LB = D_LRU // LANES
N_REST_LB = (D_IN - D_LRU) // LANES
LRU_R = 128
LRU_SUB = MIX_TS // LRU_R
RET_C = 256
OUT_TS = 1024
FFN_TS, FFN_TF = 1024, 512
FFN_HALF = 512
U_CONV_B, U_GATE_A_B, U_GATE_X_B, U_LAMBDA, U_CHUNK_DECAY, U_GN_W = 4, 5, 6, 7, 8, 9
UNIT_ROWS = 16


def _rmsnorm_rows(x, w):
    ms = jnp.mean(x * x, axis=-1, keepdims=True)
    return (x * lax.rsqrt(ms + EPS)) * w


def _silu(x):
    h = 0.5 * x
    return h * jnp.tanh(h) + h


def _params(*sem):
    return pltpu.CompilerParams(dimension_semantics=sem, vmem_limit_bytes=VMEM_LIMIT_BYTES)


def _lru_sub_block(r0, hc, slot, m, rd_lru, rd_rest, ysc_ref, sa_ref, sb_ref, cw, cb, wcat, ba, bx, lsl4):
    nt = LRU_R // SUBLANES
    base = r0 + SUBLANES
    xc = cb + rd_lru[m, pl.ds(base - 3, LRU_R), :] * cw[0:1]
    xc = xc + rd_lru[m, pl.ds(base - 2, LRU_R), :] * cw[1:2]
    xc = xc + rd_lru[m, pl.ds(base - 1, LRU_R), :] * cw[2:3]
    xc = xc + rd_lru[m, pl.ds(base, LRU_R), :] * cw[3:4]
    gates = jnp.dot(xc.astype(BF16), wcat, preferred_element_type=F32)
    tr = jnp.tanh(gates[:, :LANES] + ba)
    ti = jnp.tanh(gates[:, LANES:] + bx)
    log_a = (tr + 1.0) * lsl4
    a = jnp.exp(log_a)
    v = -jnp.tanh(log_a)
    coef = jnp.where(v > 0.0, v * lax.rsqrt((v + v) * (1.0 + v)), 0.0)
    b = coef * ((ti + 1.0) * xc)
    a3 = a.reshape(nt, SUBLANES, LANES)
    b3 = b.reshape(nt, SUBLANES, LANES)
    for s in (1, 2, 4):
        sa_ref[slot, :, SUBLANES:2 * SUBLANES, :] = a3
        sb_ref[slot, :, SUBLANES:2 * SUBLANES, :] = b3
        a_sh = sa_ref[slot, :, SUBLANES - s:2 * SUBLANES - s, :]
        b_sh = sb_ref[slot, :, SUBLANES - s:2 * SUBLANES - s, :]
        b3 = a3 * b_sh + b3
        a3 = a3 * a_sh
    tiles = []
    for t in range(nt):
        ht = a3[t] * hc + b3[t]
        hc = ht[SUBLANES - 1:SUBLANES, :]
        tiles.append(ht)
    h = jnp.concatenate(tiles, axis=0)
    g = rd_rest[m, pl.ds(r0, LRU_R), :].astype(F32)
    tg = jnp.tanh(g * (GELU_C + (GELU_C * 0.044715) * (g * g)))
    hg = 0.5 * g
    ysc_ref[m, pl.ds(r0, LRU_R), :] = (h * (hg * tg + hg)).astype(BF16)
    return hc


def _mix_body(x_hbm, ln_ref, w_ref, unit_ref, wg_ref, rot_ref, base_ref, dec_ref, xz_ref,
              fg_ref, fu_ref, fd_ref, fo_ref,
              ysc_ref, bgu_ref, bd_ref, bo_ref,
              u_ref, plru_a, plru_b, prest_a, prest_b, sa_ref, sb_ref, h_ref, state_ref,
              cq_ref, sq_ref, ck_ref, sk_ref, xbuf_ref, xsem):
    i = pl.program_id(0)
    j = pl.program_id(1)
    nt = LRU_R // SUBLANES

    for f in range(D_FF // FFN_TF):
        bgu_ref[f, :, 0:FFN_TF] = fg_ref[:, f * FFN_TF:(f + 1) * FFN_TF].astype(BF16)
        bgu_ref[f, :, FFN_TF:2 * FFN_TF] = fu_ref[:, f * FFN_TF:(f + 1) * FFN_TF].astype(BF16)
    bd_ref[...] = fd_ref[...].astype(BF16)
    bo_ref[...] = fo_ref[...].astype(BF16)

    @pl.when(jnp.logical_and(i == 0, j == 0))
    def _():
        h_ref[...] = jnp.zeros_like(h_ref)
        state_ref[...] = jnp.zeros_like(state_ref)
        plru_a[:, 0:SUBLANES, :] = jnp.zeros((N_LRU_LB + 1, SUBLANES, LANES), F32)

    def x_copy(blk):
        return pltpu.make_async_copy(x_hbm.at[pl.ds(blk * MIX_TS, MIX_TS), :], xbuf_ref, xsem)

    @pl.when(jnp.logical_and(i == 0, j == 0))
    def _():
        x_copy(0).start()

    @pl.when(jnp.logical_and(j == 0, i < N_BLK))
    def _():
        x_copy(i).wait()

        def step(r, carry):
            r0 = pl.multiple_of(r * NORM_ROWS, NORM_ROWS)
            x = xbuf_ref[pl.ds(r0, NORM_ROWS), :]
            u_ref[pl.ds(r0, NORM_ROWS), :] = _rmsnorm_rows(x, ln_ref[...]).astype(BF16)
            return carry

        lax.fori_loop(0, MIX_TS // NORM_ROWS, step, 0)

    @pl.when(jnp.logical_and(j == 1, i + 1 < N_BLK))
    def _():
        x_copy(i + 1).start()

    @pl.when(j == 0)
    def _():
        blk = jnp.maximum(i - 1, 0)
        cb_, sb_ = base_ref[blk][0:1], base_ref[blk][1:2]
        cos = cb_ * rot_ref[0] - sb_ * rot_ref[1]
        sin = sb_ * rot_ref[0] + cb_ * rot_ref[1]
        lane = lax.broadcasted_iota(jnp.int32, (1, RET_HD), 1)
        sin = jnp.where(lane < RET_HD // 2, -sin, sin)
        cq_ref[...] = cos
        sq_ref[...] = sin
        ck_ref[...] = cos * (RET_HD ** -0.5)
        sk_ref[...] = sin * (RET_HD ** -0.5)

    def head_groups(rd_lru, rd_rest, wr_lru):
        sa_ref[:, :, 0:SUBLANES, :] = jnp.ones((UNITS * LRU_SUB, nt, SUBLANES, LANES), F32)
        sb_ref[:, :, 0:SUBLANES, :] = jnp.zeros((UNITS * LRU_SUB, nt, SUBLANES, LANES), F32)
        for uu in range(UNITS):
            head_group(j * UNITS + uu, uu * LRU_SUB, rd_lru, rd_rest, wr_lru)

    def head_group(m, slot0, rd_lru, rd_rest, wr_lru):
        unit = unit_ref[m]
        cw, cb = unit[0:CONV_W], unit[U_CONV_B:U_CONV_B + 1]
        ba, bx = 0.5 * unit[U_GATE_A_B:U_GATE_A_B + 1], 0.5 * unit[U_GATE_X_B:U_GATE_X_B + 1]
        lam = unit[U_LAMBDA:U_LAMBDA + 1]
        gc, gnw = unit[U_CHUNK_DECAY:U_CHUNK_DECAY + 1], unit[U_GN_W:U_GN_W + 1]
        lsl4 = -4.0 * (jnp.maximum(-lam, 0.0) + jnp.log1p(jnp.exp(-jnp.abs(lam))))
        hc = h_ref[m][0:1, :]
        for slot in range(LRU_SUB):
            hc = _lru_sub_block(slot * LRU_R, hc, slot0 + slot, m, rd_lru, rd_rest, ysc_ref, sa_ref, sb_ref,
                                cw, cb, wg_ref[m], ba, bx, lsl4)
        h_ref[m] = jnp.broadcast_to(hc, (SUBLANES, LANES))
        wr_lru[m, 0:SUBLANES, :] = rd_lru[m, pl.ds(MIX_TS, SUBLANES), :]

        state = state_ref[m]
        for c in range(MIX_TS // RET_C):
            rows = slice(c * RET_C, (c + 1) * RET_C)
            q = rd_rest[8 + m, rows, :].astype(F32)
            k = rd_rest[16 + m, rows, :].astype(F32)
            vb = rd_rest[24 + m, rows, :]
            g = rd_rest[32 + m, rows, :].astype(F32)
            qr = q * cq_ref[rows, :] + pltpu.roll(q, RET_HD // 2, 1) * sq_ref[rows, :]
            kr = k * ck_ref[rows, :] + pltpu.roll(k, RET_HD // 2, 1) * sk_ref[rows, :]
            scores = lax.dot_general(qr.astype(BF16), kr.astype(BF16), (((1,), (1,)), ((), ())),
                                     preferred_element_type=F32)
            p = (scores * dec_ref[m]).astype(BF16)
            lhs = jnp.concatenate([p, (qr * xz_ref[0, m]).astype(BF16)], axis=1)
            rhs = jnp.concatenate([vb, state.astype(BF16)], axis=0)
            o = jnp.dot(lhs, rhs, preferred_element_type=F32)
            kz = (kr * xz_ref[1, m]).astype(BF16)
            kv = lax.dot_general(kz, vb, (((0,), (0,)), ((), ())), preferred_element_type=F32)
            state = state * gc + kv
            mu = jnp.mean(o, axis=-1, keepdims=True)
            d = o - mu
            var = jnp.mean(d * d, axis=-1, keepdims=True)
            on = (d * lax.rsqrt(var + EPS)) * gnw
            ysc_ref[N_LRU_LB + m, rows, :] = (_silu(g) * on).astype(BF16)
        state_ref[m] = state

    def projection(wr_lru, wr_rest):
        res = jnp.dot(u_ref[...], w_ref[...], preferred_element_type=F32)
        for kk in range(MIX_LB):
            lb = j * MIX_LB + kk
            blk = res[:, kk * LANES:(kk + 1) * LANES]
            wr_lru[jnp.minimum(lb, N_LRU_LB), pl.ds(SUBLANES, MIX_TS), :] = blk
            wr_rest[jnp.where(lb >= N_LRU_LB, lb - N_LRU_LB, N_REST_LB), :, :] = blk.astype(BF16)

    first, last, even = i == 0, i == N_BLK, lax.rem(i, 2) == 0
    mid = jnp.logical_not(jnp.logical_or(first, last))

    @pl.when(first)
    def _():
        projection(plru_a, prest_a)

    @pl.when(jnp.logical_and(mid, even))
    def _():
        head_groups(plru_b, prest_b, plru_a)
        projection(plru_a, prest_a)

    @pl.when(jnp.logical_and(mid, jnp.logical_not(even)))
    def _():
        head_groups(plru_a, prest_a, plru_b)
        projection(plru_b, prest_b)

    @pl.when(last)
    def _():
        head_groups(plru_b, prest_b, plru_a)


def _mix(x, ln, w_slabs, unit_tab, wcat, rot, base, decay, xz, w_gate, w_up, w_down, w_out):
    assert N_BLK % 2 == 0
    prev_blk = lambda i, j: (jnp.maximum(i - 1, 0), 0, 0, 0)
    const = lambda a: pl.BlockSpec(a.shape, lambda i, j: (0,) * a.ndim, pipeline_mode=pl.Buffered(1))
    chunk = lambda i, j, n: jnp.minimum(i * MIX_J + j, n - 1)
    nt = LRU_R // SUBLANES
    n_steps = N_BLK * MIX_J
    up_rows = D_MODEL // n_steps
    down_rows = 4 * up_rows
    n_up, n_down, n_ff = D_MODEL // up_rows, D_FF // down_rows, D_FF // FFN_TF
    up_in = pl.BlockSpec((up_rows, D_FF), lambda i, j: (chunk(i, j, n_up), 0))
    up_out = pl.BlockSpec((n_ff, up_rows, 2 * FFN_TF), lambda i, j: (0, chunk(i, j, n_up), 0))
    down_io = pl.BlockSpec((down_rows, D_MODEL), lambda i, j: (chunk(i, j, n_down), 0))
    out_io = pl.BlockSpec((up_rows, D_MODEL), lambda i, j: (chunk(i, j, n_up), 0))
    lru_buf = pltpu.VMEM((N_LRU_LB + 1, SUBLANES + MIX_TS, LANES), F32)
    rest_buf = pltpu.VMEM((N_REST_LB + 1, MIX_TS, LANES), BF16)
    return pl.pallas_call(
        _mix_body,
        grid=(N_BLK + 1, MIX_J),
        in_specs=[
            pl.BlockSpec(memory_space=pl.ANY),
            pl.BlockSpec((1, D_MODEL), lambda i, j: (0, 0)),
            pl.BlockSpec((D_MODEL, MIX_NW), lambda i, j: (0, j)),
            const(unit_tab), const(wcat), const(rot), const(base), const(decay), const(xz),
            up_in, up_in, down_io, out_io,
        ],
        out_specs=[pl.BlockSpec((None, 2 * N_LRU_LB, MIX_TS, LANES), prev_blk), up_out, down_io, out_io],
        out_shape=[
            jax.ShapeDtypeStruct((N_BLK, 2 * N_LRU_LB, MIX_TS, LANES), BF16),
            jax.ShapeDtypeStruct((n_ff, D_MODEL, 2 * FFN_TF), BF16),
            jax.ShapeDtypeStruct(w_down.shape, BF16),
            jax.ShapeDtypeStruct(w_out.shape, BF16),
        ],
        scratch_shapes=[
            pltpu.VMEM((MIX_TS, D_MODEL), BF16),
            lru_buf, lru_buf, rest_buf, rest_buf,
            pltpu.VMEM((UNITS * LRU_SUB, nt, 2 * SUBLANES, LANES), F32),
            pltpu.VMEM((UNITS * LRU_SUB, nt, 2 * SUBLANES, LANES), F32),
            pltpu.VMEM((N_LRU_LB, SUBLANES, LANES), F32),
            pltpu.VMEM((RET_HEADS, RET_HD, RET_HD), F32),
            pltpu.VMEM((MIX_TS, RET_HD), F32),
            pltpu.VMEM((MIX_TS, RET_HD), F32),
            pltpu.VMEM((MIX_TS, RET_HD), F32),
            pltpu.VMEM((MIX_TS, RET_HD), F32),
            pltpu.VMEM((MIX_TS, D_MODEL), F32),
            pltpu.SemaphoreType.DMA(()),
        ],
        compiler_params=_params("arbitrary", "arbitrary"),
        name="mix",
    )(x, ln, w_slabs, unit_tab, wcat, rot, base, decay, xz, w_gate, w_up, w_down, w_out)


def _outproj_body(x_ref, y_ref, wo_ref, o_ref):
    for rb in range(OUT_TS // MIX_TS):
        rows = slice(rb * MIX_TS, (rb + 1) * MIX_TS)
        y = jnp.concatenate([y_ref[rb, lb] for lb in range(2 * N_LRU_LB)], axis=1)
        o_ref[rows, :] = x_ref[rows, :] + jnp.dot(y, wo_ref[...], preferred_element_type=F32)


def _outproj(x, y, w_out):
    return pl.pallas_call(
        _outproj_body,
        grid=(SEQ // OUT_TS,),
        in_specs=[
            pl.BlockSpec((OUT_TS, D_MODEL), lambda i: (i, 0)),
            pl.BlockSpec((OUT_TS // MIX_TS, 2 * N_LRU_LB, MIX_TS, LANES), lambda i: (i, 0, 0, 0)),
            pl.BlockSpec((D_MODEL, D_MODEL), lambda i: (0, 0)),
        ],
        out_specs=pl.BlockSpec((OUT_TS, D_MODEL), lambda i: (i, 0)),
        out_shape=jax.ShapeDtypeStruct((SEQ, D_MODEL), F32),
        compiler_params=_params("parallel"),
        name="outproj",
    )(x, y, w_out)


def _ffn_body(h_hbm, ln2_ref, wgu_ref, wd_ref, fnw_ref, o_ref, u_ref, hbuf_ref, hsem):
    i = pl.program_id(0)
    f = pl.program_id(1)

    def h_copy(blk):
        return pltpu.make_async_copy(h_hbm.at[pl.ds(blk * FFN_TS, FFN_TS), :], hbuf_ref, hsem)

    @pl.when(jnp.logical_and(i == 0, f == 0))
    def _():
        h_copy(0).start()

    @pl.when(f == 0)
    def _():
        h_copy(i).wait()

        def step(c, carry):
            r0 = pl.multiple_of(c * NORM_ROWS, NORM_ROWS)
            hrows = hbuf_ref[pl.ds(r0, NORM_ROWS), :]
            u_ref[pl.ds(r0, NORM_ROWS), :] = _rmsnorm_rows(hrows, ln2_ref[...]).astype(BF16)
            o_ref[pl.ds(r0, NORM_ROWS), :] = hrows
            return carry

        lax.fori_loop(0, FFN_TS // NORM_ROWS, step, 0)

    @pl.when(jnp.logical_and(f == 1, i + 1 < pl.num_programs(0)))
    def _():
        h_copy(i + 1).start()

    for half in range(FFN_TS // FFN_HALF):
        rows = slice(half * FFN_HALF, (half + 1) * FFN_HALF)
        gu = jnp.dot(u_ref[rows, :], wgu_ref[...], preferred_element_type=F32)
        gate, up = gu[:, :FFN_TF], gu[:, FFN_TF:]
        act = (_silu(gate) * up).astype(BF16)
        o_ref[rows, :] += jnp.dot(act, wd_ref[...], preferred_element_type=F32)

    @pl.when(f == pl.num_programs(1) - 1)
    def _():
        def step(c, carry):
            r0 = pl.multiple_of(c * NORM_ROWS, NORM_ROWS)
            rows = pl.ds(r0, NORM_ROWS)
            o_ref[rows, :] = _rmsnorm_rows(o_ref[rows, :], fnw_ref[...])
            return carry

        lax.fori_loop(0, FFN_TS // NORM_ROWS, step, 0)


def _ffn(h1, ln2, wgu, wd, fnw):
    return pl.pallas_call(
        _ffn_body,
        grid=(SEQ // FFN_TS, D_FF // FFN_TF),
        in_specs=[
            pl.BlockSpec(memory_space=pl.ANY),
            pl.BlockSpec((1, D_MODEL), lambda i, f: (0, 0)),
            pl.BlockSpec((None, D_MODEL, 2 * FFN_TF), lambda i, f: (f, 0, 0)),
            pl.BlockSpec((FFN_TF, D_MODEL), lambda i, f: (f, 0)),
            pl.BlockSpec((1, D_MODEL), lambda i, f: (0, 0)),
        ],
        out_specs=pl.BlockSpec((FFN_TS, D_MODEL), lambda i, f: (i, 0)),
        out_shape=jax.ShapeDtypeStruct((SEQ, D_MODEL), F32),
        scratch_shapes=[
            pltpu.VMEM((FFN_TS, D_MODEL), BF16),
            pltpu.VMEM((FFN_TS, D_MODEL), F32),
            pltpu.SemaphoreType.DMA(()),
        ],
        compiler_params=_params("arbitrary", "arbitrary"),
        name="ffn",
    )(h1, ln2, wgu, wd, fnw)


def _retention_tables():
    H, Dh, C = RET_HEADS, RET_HD, RET_C
    inv_freq = ROPE_BASE ** (-np.arange(0, Dh, 2, dtype=np.float64) / Dh)
    inv2 = np.concatenate([inv_freq, inv_freq])
    off = np.arange(MIX_TS, dtype=np.float64)[:, None] * inv2[None, :]
    base = (np.arange(N_BLK, dtype=np.float64) * MIX_TS)[:, None, None] * inv2[None, None, :]
    log_gamma = np.log1p(-np.exp2(-5.0 - np.arange(H, dtype=np.float64)))
    idx = np.arange(C)
    diff = idx[:, None] - idx[None, :]
    decay = np.where(diff >= 0, np.exp(log_gamma[:, None, None] * np.maximum(diff, 0)[None]), 0.0)
    zeta = np.exp(log_gamma[:, None] * (C - 1 - idx)[None, :])
    xi = np.exp(log_gamma[:, None] * (idx + 1)[None, :])
    gc = np.exp(log_gamma * C)
    per_head = lambda t: np.broadcast_to(t[:, :, None], (H, C, Dh))
    tables = (np.stack([np.cos(off), np.sin(off)]),
              np.concatenate([np.cos(base), np.sin(base)], axis=1),
              decay,
              np.stack([per_head(xi), per_head(zeta)]),
              np.broadcast_to(gc[:, None, None], (H, 1, Dh)))
    return tuple(jnp.asarray(np.ascontiguousarray(t, dtype=np.float32)) for t in tables)


def kernel(x, ln1_w, w_in, conv_w, conv_b, gate_a_w, gate_a_b, gate_x_w, gate_x_b, lru_lambda, ret_gn_w,
           w_out, ln2_w, w_ffn_gate, w_ffn_up, w_ffn_down, final_norm_w):
    x2 = x.reshape(SEQ, D_MODEL)
    row = lambda v: v.reshape(1, -1)
    by_lb = lambda v: v.reshape(-1, 1, LANES)
    wcat = (0.5 * jnp.concatenate([gate_a_w[0], gate_x_w[0]], axis=-1)).astype(BF16)
    cw = conv_w[0].reshape(CONV_W, N_LRU_LB, LANES).transpose(1, 0, 2)
    rot, base, decay, xz, gc = _retention_tables()
    unit_tab = jnp.concatenate(
        [cw, by_lb(conv_b[0]), by_lb(gate_a_b[0]), by_lb(gate_x_b[0]), by_lb(lru_lambda[0]), gc,
         by_lb(ret_gn_w[0]), jnp.zeros((N_LRU_LB, UNIT_ROWS - U_GN_W - 1, LANES), F32)], axis=1)
    y, wgu, wd, wo = _mix(x2, row(ln1_w[0]), w_in[0].astype(BF16), unit_tab, wcat, rot, base, decay, xz,
                          w_ffn_gate[0], w_ffn_up[0], w_ffn_down[0], w_out[0])
    h1 = _outproj(x2, y, wo)
    out = _ffn(h1, row(ln2_w[0]), wgu, wd, row(final_norm_w))
    return out.reshape(1, SEQ, D_MODEL)
```

```python
import jax
import jax.numpy as jnp
import numpy as np
from jax import lax
from jax.experimental import pallas as pl
from jax.experimental.pallas import tpu as pltpu

D_MODEL = 2048
SEQ = 8192
D_LRU = 1024
D_RET = 1024
CONV_W = 4
RET_HEADS = 8
RET_HD = 128
ROPE_BASE = 10000.0
D_FF = 5632
D_IN = 6144
EPS = 1e-6
GELU_C = 0.7978845608028654

SUBLANES = 8
LANES = 128
VMEM_LIMIT_BYTES = 56 * 1024 * 1024

F32 = jnp.float32
BF16 = jnp.bfloat16

NORM_ROWS = 128
MIX_TS = 512
MIX_J = 4
MIX_NW = D_IN // MIX_J
MIX_LB = MIX_NW // LANES
N_BLK = SEQ // MIX_TS
N_LRU_LB = D_LRU // LANES
UNITS = N_LRU_LB // MIX_J
N_REST_LB = (D_IN - D_LRU) // LANES
LRU_R = 128
LRU_SUB = MIX_TS // LRU_R
RET_C = 256
OUT_TS = 1024
FFN_TS, FFN_TF = 1024, 512
FFN_HALF = 512
U_CONV_B, U_GATE_A_B, U_GATE_X_B, U_LAMBDA, U_CHUNK_DECAY, U_GN_W = 4, 5, 6, 7, 8, 9
UNIT_ROWS = 16


def _rmsnorm_rows(x, w):
    ms = jnp.mean(x * x, axis=-1, keepdims=True)
    return (x * lax.rsqrt(ms + EPS)) * w


def _silu(x):
    h = 0.5 * x
    return h * jnp.tanh(h) + h


def _params(*sem):
    return pltpu.CompilerParams(dimension_semantics=sem, vmem_limit_bytes=VMEM_LIMIT_BYTES)


def _lru_sub_block(r0, hc, slot, m, rd_lru, rd_rest, y_ref, sa_ref, sb_ref, cw, cb, wcat, ba, bx, lsl4):
    nt = LRU_R // SUBLANES
    base = r0 + SUBLANES
    xc = cb + rd_lru[m, pl.ds(base - 3, LRU_R), :] * cw[0:1]
    xc = xc + rd_lru[m, pl.ds(base - 2, LRU_R), :] * cw[1:2]
    xc = xc + rd_lru[m, pl.ds(base - 1, LRU_R), :] * cw[2:3]
    xc = xc + rd_lru[m, pl.ds(base, LRU_R), :] * cw[3:4]
    gates = jnp.dot(xc.astype(BF16), wcat, preferred_element_type=F32)
    tr = jnp.tanh(gates[:, :LANES] + ba)
    ti = jnp.tanh(gates[:, LANES:] + bx)
    log_a = (tr + 1.0) * lsl4
    a = jnp.exp(log_a)
    v = -jnp.tanh(log_a)
    coef = jnp.where(v > 0.0, v * lax.rsqrt((v + v) * (1.0 + v)), 0.0)
    b = coef * ((ti + 1.0) * xc)
    a3 = a.reshape(nt, SUBLANES, LANES)
    b3 = b.reshape(nt, SUBLANES, LANES)
    for s in (1, 2, 4):
        sa_ref[slot, :, SUBLANES:2 * SUBLANES, :] = a3
        sb_ref[slot, :, SUBLANES:2 * SUBLANES, :] = b3
        a_sh = sa_ref[slot, :, SUBLANES - s:2 * SUBLANES - s, :]
        b_sh = sb_ref[slot, :, SUBLANES - s:2 * SUBLANES - s, :]
        b3 = a3 * b_sh + b3
        a3 = a3 * a_sh
    tiles = []
    for t in range(nt):
        ht = a3[t] * hc + b3[t]
        hc = ht[SUBLANES - 1:SUBLANES, :]
        tiles.append(ht)
    h = jnp.concatenate(tiles, axis=0)
    g = rd_rest[m, pl.ds(r0, LRU_R), :].astype(F32)
    tg = jnp.tanh(g * (GELU_C + (GELU_C * 0.044715) * (g * g)))
    hg = 0.5 * g
    y_ref[m, pl.ds(r0, LRU_R), :] = (h * (hg * tg + hg)).astype(BF16)
    return hc


def _mix_body(x_hbm, ln_ref, w_ref, unit_ref, wg_ref, rot_ref, base_ref, dec_ref, xz_ref,
              fg_ref, fu_ref, fd_ref, fo_ref,
              y_ref, bgu_ref, bd_ref, bo_ref,
              u_ref, plru_a, plru_b, prest_a, prest_b, sa_ref, sb_ref, h_ref, state_ref,
              cq_ref, sq_ref, ck_ref, sk_ref, xbuf_ref, xsem):
    i = pl.program_id(0)
    j = pl.program_id(1)
    nt = LRU_R // SUBLANES

    for f in range(D_FF // FFN_TF):
        bgu_ref[f, :, 0:FFN_TF] = fg_ref[:, f * FFN_TF:(f + 1) * FFN_TF].astype(BF16)
        bgu_ref[f, :, FFN_TF:2 * FFN_TF] = fu_ref[:, f * FFN_TF:(f + 1) * FFN_TF].astype(BF16)
    bd_ref[...] = fd_ref[...].astype(BF16)
    bo_ref[...] = fo_ref[...].astype(BF16)

    @pl.when(jnp.logical_and(i == 0, j == 0))
    def _():
        h_ref[...] = jnp.zeros_like(h_ref)
        state_ref[...] = jnp.zeros_like(state_ref)
        plru_a[:, 0:SUBLANES, :] = jnp.zeros((N_LRU_LB + 1, SUBLANES, LANES), F32)

    def x_copy(blk):
        return pltpu.make_async_copy(x_hbm.at[pl.ds(blk * MIX_TS, MIX_TS), :], xbuf_ref, xsem)

    @pl.when(jnp.logical_and(i == 0, j == 0))
    def _():
        x_copy(0).start()

    @pl.when(jnp.logical_and(j == 0, i < N_BLK))
    def _():
        x_copy(i).wait()

        def step(r, carry):
            r0 = pl.multiple_of(r * NORM_ROWS, NORM_ROWS)
            x = xbuf_ref[pl.ds(r0, NORM_ROWS), :]
            u_ref[pl.ds(r0, NORM_ROWS), :] = _rmsnorm_rows(x, ln_ref[...]).astype(BF16)
            return carry

        lax.fori_loop(0, MIX_TS // NORM_ROWS, step, 0)

    @pl.when(jnp.logical_and(j == 1, i + 1 < N_BLK))
    def _():
        x_copy(i + 1).start()

    @pl.when(j == 0)
    def _():
        blk = jnp.maximum(i - 1, 0)
        cb_, sb_ = base_ref[blk][0:1], base_ref[blk][1:2]
        cos = cb_ * rot_ref[0] - sb_ * rot_ref[1]
        sin = sb_ * rot_ref[0] + cb_ * rot_ref[1]
        lane = lax.broadcasted_iota(jnp.int32, (1, RET_HD), 1)
        sin = jnp.where(lane < RET_HD // 2, -sin, sin)
        cq_ref[...] = cos
        sq_ref[...] = sin
        ck_ref[...] = cos * (RET_HD ** -0.5)
        sk_ref[...] = sin * (RET_HD ** -0.5)

    def head_groups(rd_lru, rd_rest, wr_lru):
        sa_ref[:, :, 0:SUBLANES, :] = jnp.ones((UNITS * LRU_SUB, nt, SUBLANES, LANES), F32)
        sb_ref[:, :, 0:SUBLANES, :] = jnp.zeros((UNITS * LRU_SUB, nt, SUBLANES, LANES), F32)
        for uu in range(UNITS):
            head_group(j * UNITS + uu, uu * LRU_SUB, rd_lru, rd_rest, wr_lru)

    def head_group(m, slot0, rd_lru, rd_rest, wr_lru):
        unit = unit_ref[m]
        cw, cb = unit[0:CONV_W], unit[U_CONV_B:U_CONV_B + 1]
        ba, bx = 0.5 * unit[U_GATE_A_B:U_GATE_A_B + 1], 0.5 * unit[U_GATE_X_B:U_GATE_X_B + 1]
        lam = unit[U_LAMBDA:U_LAMBDA + 1]
        gc, gnw = unit[U_CHUNK_DECAY:U_CHUNK_DECAY + 1], unit[U_GN_W:U_GN_W + 1]
        lsl4 = -4.0 * (jnp.maximum(-lam, 0.0) + jnp.log1p(jnp.exp(-jnp.abs(lam))))
        hc = h_ref[m][0:1, :]
        for slot in range(LRU_SUB):
            hc = _lru_sub_block(slot * LRU_R, hc, slot0 + slot, m, rd_lru, rd_rest, y_ref, sa_ref, sb_ref,
                                cw, cb, wg_ref[m], ba, bx, lsl4)
        h_ref[m] = jnp.broadcast_to(hc, (SUBLANES, LANES))
        wr_lru[m, 0:SUBLANES, :] = rd_lru[m, pl.ds(MIX_TS, SUBLANES), :]

        state = state_ref[m]
        for c in range(MIX_TS // RET_C):
            rows = slice(c * RET_C, (c + 1) * RET_C)
            q = rd_rest[8 + m, rows, :].astype(F32)
            k = rd_rest[16 + m, rows, :].astype(F32)
            vb = rd_rest[24 + m, rows, :]
            g = rd_rest[32 + m, rows, :].astype(F32)
            qr = q * cq_ref[rows, :] + pltpu.roll(q, RET_HD // 2, 1) * sq_ref[rows, :]
            kr = k * ck_ref[rows, :] + pltpu.roll(k, RET_HD // 2, 1) * sk_ref[rows, :]
            scores = lax.dot_general(qr.astype(BF16), kr.astype(BF16), (((1,), (1,)), ((), ())),
                                     preferred_element_type=F32)
            p = (scores * dec_ref[m]).astype(BF16)
            lhs = jnp.concatenate([p, (qr * xz_ref[0, m]).astype(BF16)], axis=1)
            rhs = jnp.concatenate([vb, state.astype(BF16)], axis=0)
            o = jnp.dot(lhs, rhs, preferred_element_type=F32)
            kz = (kr * xz_ref[1, m]).astype(BF16)
            kv = lax.dot_general(kz, vb, (((0,), (0,)), ((), ())), preferred_element_type=F32)
            state = state * gc + kv
            mu = jnp.mean(o, axis=-1, keepdims=True)
            d = o - mu
            var = jnp.mean(d * d, axis=-1, keepdims=True)
            on = (d * lax.rsqrt(var + EPS)) * gnw
            y_ref[N_LRU_LB + m, rows, :] = (_silu(g) * on).astype(BF16)
        state_ref[m] = state

    def projection(wr_lru, wr_rest):
        res = jnp.dot(u_ref[...], w_ref[...], preferred_element_type=F32)
        for kk in range(MIX_LB):
            lb = j * MIX_LB + kk
            blk = res[:, kk * LANES:(kk + 1) * LANES]
            wr_lru[jnp.minimum(lb, N_LRU_LB), pl.ds(SUBLANES, MIX_TS), :] = blk
            wr_rest[jnp.where(lb >= N_LRU_LB, lb - N_LRU_LB, N_REST_LB), :, :] = blk.astype(BF16)

    first, last, even = i == 0, i == N_BLK, lax.rem(i, 2) == 0
    mid = jnp.logical_not(jnp.logical_or(first, last))

    @pl.when(first)
    def _():
        projection(plru_a, prest_a)

    @pl.when(jnp.logical_and(mid, even))
    def _():
        head_groups(plru_b, prest_b, plru_a)
        projection(plru_a, prest_a)

    @pl.when(jnp.logical_and(mid, jnp.logical_not(even)))
    def _():
        head_groups(plru_a, prest_a, plru_b)
        projection(plru_b, prest_b)

    @pl.when(last)
    def _():
        head_groups(plru_b, prest_b, plru_a)


def _mix(x, ln, w_slabs, unit_tab, wcat, rot, base, decay, xz, w_gate, w_up, w_down, w_out):
    assert N_BLK % 2 == 0
    prev_blk = lambda i, j: (jnp.maximum(i - 1, 0), 0, 0, 0)
    const = lambda a: pl.BlockSpec(a.shape, lambda i, j: (0,) * a.ndim, pipeline_mode=pl.Buffered(1))
    chunk = lambda i, j, n: jnp.minimum(i * MIX_J + j, n - 1)
    nt = LRU_R // SUBLANES
    n_steps = N_BLK * MIX_J
    up_rows = D_MODEL // n_steps
    down_rows = 4 * up_rows
    n_up, n_down, n_ff = D_MODEL // up_rows, D_FF // down_rows, D_FF // FFN_TF
    up_in = pl.BlockSpec((up_rows, D_FF), lambda i, j: (chunk(i, j, n_up), 0))
    up_out = pl.BlockSpec((n_ff, up_rows, 2 * FFN_TF), lambda i, j: (0, chunk(i, j, n_up), 0))
    down_io = pl.BlockSpec((down_rows, D_MODEL), lambda i, j: (chunk(i, j, n_down), 0))
    out_io = pl.BlockSpec((up_rows, D_MODEL), lambda i, j: (chunk(i, j, n_up), 0))
    lru_buf = pltpu.VMEM((N_LRU_LB + 1, SUBLANES + MIX_TS, LANES), F32)
    rest_buf = pltpu.VMEM((N_REST_LB + 1, MIX_TS, LANES), BF16)
    return pl.pallas_call(
        _mix_body,
        grid=(N_BLK + 1, MIX_J),
        in_specs=[
            pl.BlockSpec(memory_space=pl.ANY),
            pl.BlockSpec((1, D_MODEL), lambda i, j: (0, 0)),
            pl.BlockSpec((D_MODEL, MIX_NW), lambda i, j: (0, j)),
            const(unit_tab), const(wcat), const(rot), const(base), const(decay), const(xz),
            up_in, up_in, down_io, out_io,
        ],
        out_specs=[pl.BlockSpec((None, 2 * N_LRU_LB, MIX_TS, LANES), prev_blk), up_out, down_io, out_io],
        out_shape=[
            jax.ShapeDtypeStruct((N_BLK, 2 * N_LRU_LB, MIX_TS, LANES), BF16),
            jax.ShapeDtypeStruct((n_ff, D_MODEL, 2 * FFN_TF), BF16),
            jax.ShapeDtypeStruct(w_down.shape, BF16),
            jax.ShapeDtypeStruct(w_out.shape, BF16),
        ],
        scratch_shapes=[
            pltpu.VMEM((MIX_TS, D_MODEL), BF16),
            lru_buf, lru_buf, rest_buf, rest_buf,
            pltpu.VMEM((UNITS * LRU_SUB, nt, 2 * SUBLANES, LANES), F32),
            pltpu.VMEM((UNITS * LRU_SUB, nt, 2 * SUBLANES, LANES), F32),
            pltpu.VMEM((N_LRU_LB, SUBLANES, LANES), F32),
            pltpu.VMEM((RET_HEADS, RET_HD, RET_HD), F32),
            pltpu.VMEM((MIX_TS, RET_HD), F32),
            pltpu.VMEM((MIX_TS, RET_HD), F32),
            pltpu.VMEM((MIX_TS, RET_HD), F32),
            pltpu.VMEM((MIX_TS, RET_HD), F32),
            pltpu.VMEM((MIX_TS, D_MODEL), F32),
            pltpu.SemaphoreType.DMA(()),
        ],
        compiler_params=_params("arbitrary", "arbitrary"),
        name="mix",
    )(x, ln, w_slabs, unit_tab, wcat, rot, base, decay, xz, w_gate, w_up, w_down, w_out)


def _outproj_body(y_ref, wo_ref, o_ref):
    for rb in range(OUT_TS // MIX_TS):
        rows = slice(rb * MIX_TS, (rb + 1) * MIX_TS)
        y = jnp.concatenate([y_ref[rb, lb] for lb in range(2 * N_LRU_LB)], axis=1)
        o_ref[rows, :] = jnp.dot(y, wo_ref[...], preferred_element_type=F32)


def _outproj(y, w_out):
    return pl.pallas_call(
        _outproj_body,
        grid=(SEQ // OUT_TS,),
        in_specs=[
            pl.BlockSpec((OUT_TS // MIX_TS, 2 * N_LRU_LB, MIX_TS, LANES), lambda i: (i, 0, 0, 0)),
            pl.BlockSpec((D_MODEL, D_MODEL), lambda i: (0, 0)),
        ],
        out_specs=pl.BlockSpec((OUT_TS, D_MODEL), lambda i: (i, 0)),
        out_shape=jax.ShapeDtypeStruct((SEQ, D_MODEL), F32),
        compiler_params=_params("parallel"),
        name="outproj",
    )(y, w_out)


def _ffn_body(x_hbm, a_hbm, ln2_ref, wgu_ref, wd_ref, fnw_ref, o_ref, u_ref, xbuf_ref, abuf_ref, sems):
    i = pl.program_id(0)
    f = pl.program_id(1)

    def copies(blk):
        rows = pl.ds(blk * FFN_TS, FFN_TS)
        return (pltpu.make_async_copy(x_hbm.at[rows, :], xbuf_ref, sems.at[0]),
                pltpu.make_async_copy(a_hbm.at[rows, :], abuf_ref, sems.at[1]))

    @pl.when(jnp.logical_and(i == 0, f == 0))
    def _():
        for cp in copies(0):
            cp.start()

    @pl.when(f == 0)
    def _():
        for cp in copies(i):
            cp.wait()

        def step(c, carry):
            r0 = pl.multiple_of(c * NORM_ROWS, NORM_ROWS)
            hrows = xbuf_ref[pl.ds(r0, NORM_ROWS), :] + abuf_ref[pl.ds(r0, NORM_ROWS), :]
            u_ref[pl.ds(r0, NORM_ROWS), :] = _rmsnorm_rows(hrows, ln2_ref[...]).astype(BF16)
            o_ref[pl.ds(r0, NORM_ROWS), :] = hrows
            return carry

        lax.fori_loop(0, FFN_TS // NORM_ROWS, step, 0)

    @pl.when(jnp.logical_and(f == 1, i + 1 < pl.num_programs(0)))
    def _():
        for cp in copies(i + 1):
            cp.start()

    for half in range(FFN_TS // FFN_HALF):
        rows = slice(half * FFN_HALF, (half + 1) * FFN_HALF)
        gu = jnp.dot(u_ref[rows, :], wgu_ref[...], preferred_element_type=F32)
        gate, up = gu[:, :FFN_TF], gu[:, FFN_TF:]
        act = (_silu(gate) * up).astype(BF16)
        o_ref[rows, :] += jnp.dot(act, wd_ref[...], preferred_element_type=F32)

    @pl.when(f == pl.num_programs(1) - 1)
    def _():
        def step(c, carry):
            r0 = pl.multiple_of(c * NORM_ROWS, NORM_ROWS)
            rows = pl.ds(r0, NORM_ROWS)
            o_ref[rows, :] = _rmsnorm_rows(o_ref[rows, :], fnw_ref[...])
            return carry

        lax.fori_loop(0, FFN_TS // NORM_ROWS, step, 0)


def _ffn(x, a, ln2, wgu, wd, fnw):
    return pl.pallas_call(
        _ffn_body,
        grid=(SEQ // FFN_TS, D_FF // FFN_TF),
        in_specs=[
            pl.BlockSpec(memory_space=pl.ANY),
            pl.BlockSpec(memory_space=pl.ANY),
            pl.BlockSpec((1, D_MODEL), lambda i, f: (0, 0)),
            pl.BlockSpec((None, D_MODEL, 2 * FFN_TF), lambda i, f: (f, 0, 0)),
            pl.BlockSpec((FFN_TF, D_MODEL), lambda i, f: (f, 0)),
            pl.BlockSpec((1, D_MODEL), lambda i, f: (0, 0)),
        ],
        out_specs=pl.BlockSpec((FFN_TS, D_MODEL), lambda i, f: (i, 0)),
        out_shape=jax.ShapeDtypeStruct((SEQ, D_MODEL), F32),
        scratch_shapes=[
            pltpu.VMEM((FFN_TS, D_MODEL), BF16),
            pltpu.VMEM((FFN_TS, D_MODEL), F32),
            pltpu.VMEM((FFN_TS, D_MODEL), F32),
            pltpu.SemaphoreType.DMA((2,)),
        ],
        compiler_params=_params("arbitrary", "arbitrary"),
        name="ffn",
    )(x, a, ln2, wgu, wd, fnw)


def _retention_tables():
    H, Dh, C = RET_HEADS, RET_HD, RET_C
    inv_freq = ROPE_BASE ** (-np.arange(0, Dh, 2, dtype=np.float64) / Dh)
    inv2 = np.concatenate([inv_freq, inv_freq])
    off = np.arange(MIX_TS, dtype=np.float64)[:, None] * inv2[None, :]
    base = (np.arange(N_BLK, dtype=np.float64) * MIX_TS)[:, None, None] * inv2[None, None, :]
    log_gamma = np.log1p(-np.exp2(-5.0 - np.arange(H, dtype=np.float64)))
    idx = np.arange(C)
    diff = idx[:, None] - idx[None, :]
    decay = np.where(diff >= 0, np.exp(log_gamma[:, None, None] * np.maximum(diff, 0)[None]), 0.0)
    zeta = np.exp(log_gamma[:, None] * (C - 1 - idx)[None, :])
    xi = np.exp(log_gamma[:, None] * (idx + 1)[None, :])
    gc = np.exp(log_gamma * C)
    per_head = lambda t: np.broadcast_to(t[:, :, None], (H, C, Dh))
    tables = (np.stack([np.cos(off), np.sin(off)]),
              np.concatenate([np.cos(base), np.sin(base)], axis=1),
              decay,
              np.stack([per_head(xi), per_head(zeta)]),
              np.broadcast_to(gc[:, None, None], (H, 1, Dh)))
    return tuple(jnp.asarray(np.ascontiguousarray(t, dtype=np.float32)) for t in tables)


def kernel(x, ln1_w, w_in, conv_w, conv_b, gate_a_w, gate_a_b, gate_x_w, gate_x_b, lru_lambda, ret_gn_w,
           w_out, ln2_w, w_ffn_gate, w_ffn_up, w_ffn_down, final_norm_w):
    x2 = x.reshape(SEQ, D_MODEL)
    row = lambda v: v.reshape(1, -1)
    by_lb = lambda v: v.reshape(-1, 1, LANES)
    wcat = (0.5 * jnp.concatenate([gate_a_w[0], gate_x_w[0]], axis=-1)).astype(BF16)
    cw = conv_w[0].reshape(CONV_W, N_LRU_LB, LANES).transpose(1, 0, 2)
    rot, base, decay, xz, gc = _retention_tables()
    unit_tab = jnp.concatenate(
        [cw, by_lb(conv_b[0]), by_lb(gate_a_b[0]), by_lb(gate_x_b[0]), by_lb(lru_lambda[0]), gc,
         by_lb(ret_gn_w[0]), jnp.zeros((N_LRU_LB, UNIT_ROWS - U_GN_W - 1, LANES), F32)], axis=1)
    y, wgu, wd, wo = _mix(x2, row(ln1_w[0]), w_in[0].astype(BF16), unit_tab, wcat, rot, base, decay, xz,
                          w_ffn_gate[0], w_ffn_up[0], w_ffn_down[0], w_out[0])
    a = _outproj(y, wo)
    out = _ffn(x2, a, row(ln2_w[0]), wgu, wd, row(final_norm_w))
    return out.reshape(1, SEQ, D_MODEL)
```

```python
import jax
import jax.numpy as jnp
import numpy as np
from jax import lax
from jax.experimental import pallas as pl
from jax.experimental.pallas import tpu as pltpu

D_MODEL = 2048
SEQ = 8192
D_LRU = 1024
D_RET = 1024
CONV_W = 4
RET_HEADS = 8
RET_HD = 128
ROPE_BASE = 10000.0
D_FF = 5632
D_IN = 6144
EPS = 1e-6
GELU_C = 0.7978845608028654

SUBLANES = 8
LANES = 128
VMEM_LIMIT_BYTES = 56 * 1024 * 1024

F32 = jnp.float32
BF16 = jnp.bfloat16

NORM_ROWS = 128
MIX_TS = 512
MIX_J = 4
MIX_NW = D_IN // MIX_J
MIX_LB = MIX_NW // LANES
N_BLK = SEQ // MIX_TS
N_LRU_LB = D_LRU // LANES
UNITS = N_LRU_LB // MIX_J
N_REST_LB = (D_IN - D_LRU) // LANES
LRU_R = 128
LRU_SUB = MIX_TS // LRU_R
RET_C = 256
OUT_TS = 1024
FFN_TS, FFN_TF = 1024, 512
FFN_HALF = 512
U_CONV_B, U_GATE_A_B, U_GATE_X_B, U_LAMBDA, U_CHUNK_DECAY, U_GN_W = 4, 5, 6, 7, 8, 9
UNIT_ROWS = 16


def _rmsnorm_rows(x, w):
    ms = jnp.mean(x * x, axis=-1, keepdims=True)
    return (x * lax.rsqrt(ms + EPS)) * w


def _silu(x):
    h = 0.5 * x
    return h * jnp.tanh(h) + h


def _params(*sem):
    return pltpu.CompilerParams(dimension_semantics=sem, vmem_limit_bytes=VMEM_LIMIT_BYTES)


def _lru_sub_block(r0, hc, slot, m, rd_lru, rd_rest, y_ref, sa_ref, sb_ref, cw, cb, wcat, ba, bx, lsl4):
    nt = LRU_R // SUBLANES
    base = r0 + SUBLANES
    xc = cb + rd_lru[m, pl.ds(base - 3, LRU_R), :] * cw[0:1]
    xc = xc + rd_lru[m, pl.ds(base - 2, LRU_R), :] * cw[1:2]
    xc = xc + rd_lru[m, pl.ds(base - 1, LRU_R), :] * cw[2:3]
    xc = xc + rd_lru[m, pl.ds(base, LRU_R), :] * cw[3:4]
    gates = jnp.dot(xc.astype(BF16), wcat, preferred_element_type=F32)
    tr = jnp.tanh(gates[:, :LANES] + ba)
    ti = jnp.tanh(gates[:, LANES:] + bx)
    log_a = (tr + 1.0) * lsl4
    a = jnp.exp(log_a)
    v = -jnp.tanh(log_a)
    coef = jnp.where(v > 0.0, v * lax.rsqrt((v + v) * (1.0 + v)), 0.0)
    b = coef * ((ti + 1.0) * xc)
    a3 = a.reshape(nt, SUBLANES, LANES)
    b3 = b.reshape(nt, SUBLANES, LANES)
    for s in (1, 2, 4):
        sa_ref[slot, :, SUBLANES:2 * SUBLANES, :] = a3
        sb_ref[slot, :, SUBLANES:2 * SUBLANES, :] = b3
        a_sh = sa_ref[slot, :, SUBLANES - s:2 * SUBLANES - s, :]
        b_sh = sb_ref[slot, :, SUBLANES - s:2 * SUBLANES - s, :]
        b3 = a3 * b_sh + b3
        a3 = a3 * a_sh
    tiles = []
    for t in range(nt):
        ht = a3[t] * hc + b3[t]
        hc = ht[SUBLANES - 1:SUBLANES, :]
        tiles.append(ht)
    h = jnp.concatenate(tiles, axis=0)
    g = rd_rest[m, pl.ds(r0, LRU_R), :].astype(F32)
    tg = jnp.tanh(g * (GELU_C + (GELU_C * 0.044715) * (g * g)))
    hg = 0.5 * g
    y_ref[m, pl.ds(r0, LRU_R), :] = (h * (hg * tg + hg)).astype(BF16)
    return hc


def _mix_body(x_hbm, ln_ref, w_ref, unit_ref, wg_ref, rot_ref, base_ref, dec_ref, xz_ref,
              fg_ref, fu_ref, fd_ref, fo_ref,
              y_ref, bgu_ref, bd_ref, bo_ref,
              u_ref, plru_a, plru_b, prest_a, prest_b, sa_ref, sb_ref, h_ref, state_ref,
              cq_ref, sq_ref, ck_ref, sk_ref, xbuf_ref, xsem):
    i = pl.program_id(0)
    j = pl.program_id(1)
    nt = LRU_R // SUBLANES

    for f in range(D_FF // FFN_TF):
        bgu_ref[f, :, 0:FFN_TF] = fg_ref[:, f * FFN_TF:(f + 1) * FFN_TF].astype(BF16)
        bgu_ref[f, :, FFN_TF:2 * FFN_TF] = fu_ref[:, f * FFN_TF:(f + 1) * FFN_TF].astype(BF16)
    bd_ref[...] = fd_ref[...].astype(BF16)
    bo_ref[...] = fo_ref[...].astype(BF16)

    @pl.when(jnp.logical_and(i == 0, j == 0))
    def _():
        h_ref[...] = jnp.zeros_like(h_ref)
        state_ref[...] = jnp.zeros_like(state_ref)
        plru_a[:, 0:SUBLANES, :] = jnp.zeros((N_LRU_LB + 1, SUBLANES, LANES), F32)

    def x_copy(blk):
        return pltpu.make_async_copy(x_hbm.at[pl.ds(blk * MIX_TS, MIX_TS), :], xbuf_ref, xsem)

    @pl.when(jnp.logical_and(i == 0, j == 0))
    def _():
        x_copy(0).start()

    @pl.when(jnp.logical_and(j == 0, i < N_BLK))
    def _():
        x_copy(i).wait()

        def step(r, carry):
            r0 = pl.multiple_of(r * NORM_ROWS, NORM_ROWS)
            x = xbuf_ref[pl.ds(r0, NORM_ROWS), :]
            u_ref[pl.ds(r0, NORM_ROWS), :] = _rmsnorm_rows(x, ln_ref[...]).astype(BF16)
            return carry

        lax.fori_loop(0, MIX_TS // NORM_ROWS, step, 0)

    @pl.when(jnp.logical_and(j == 1, i + 1 < N_BLK))
    def _():
        x_copy(i + 1).start()

    @pl.when(j == 0)
    def _():
        blk = jnp.maximum(i - 1, 0)
        cb_, sb_ = base_ref[blk][0:1], base_ref[blk][1:2]
        cos = cb_ * rot_ref[0] - sb_ * rot_ref[1]
        sin = sb_ * rot_ref[0] + cb_ * rot_ref[1]
        lane = lax.broadcasted_iota(jnp.int32, (1, RET_HD), 1)
        sin = jnp.where(lane < RET_HD // 2, -sin, sin)
        cq_ref[...] = cos
        sq_ref[...] = sin
        ck_ref[...] = cos * (RET_HD ** -0.5)
        sk_ref[...] = sin * (RET_HD ** -0.5)

    def head_groups(rd_lru, rd_rest, wr_lru):
        sa_ref[:, :, 0:SUBLANES, :] = jnp.ones((UNITS * LRU_SUB, nt, SUBLANES, LANES), F32)
        sb_ref[:, :, 0:SUBLANES, :] = jnp.zeros((UNITS * LRU_SUB, nt, SUBLANES, LANES), F32)
        for uu in range(UNITS):
            head_group(j * UNITS + uu, uu * LRU_SUB, rd_lru, rd_rest, wr_lru)

    def head_group(m, slot0, rd_lru, rd_rest, wr_lru):
        unit = unit_ref[m]
        cw, cb = unit[0:CONV_W], unit[U_CONV_B:U_CONV_B + 1]
        ba, bx = 0.5 * unit[U_GATE_A_B:U_GATE_A_B + 1], 0.5 * unit[U_GATE_X_B:U_GATE_X_B + 1]
        lam = unit[U_LAMBDA:U_LAMBDA + 1]
        gc, gnw = unit[U_CHUNK_DECAY:U_CHUNK_DECAY + 1], unit[U_GN_W:U_GN_W + 1]
        lsl4 = -4.0 * (jnp.maximum(-lam, 0.0) + jnp.log1p(jnp.exp(-jnp.abs(lam))))
        hc = h_ref[m][0:1, :]
        for slot in range(LRU_SUB):
            hc = _lru_sub_block(slot * LRU_R, hc, slot0 + slot, m, rd_lru, rd_rest, y_ref, sa_ref, sb_ref,
                                cw, cb, wg_ref[m], ba, bx, lsl4)
        h_ref[m] = jnp.broadcast_to(hc, (SUBLANES, LANES))
        wr_lru[m, 0:SUBLANES, :] = rd_lru[m, pl.ds(MIX_TS, SUBLANES), :]

        state = state_ref[m]
        for c in range(MIX_TS // RET_C):
            rows = slice(c * RET_C, (c + 1) * RET_C)
            q = rd_rest[8 + m, rows, :].astype(F32)
            k = rd_rest[16 + m, rows, :].astype(F32)
            vb = rd_rest[24 + m, rows, :]
            g = rd_rest[32 + m, rows, :].astype(F32)
            qr = q * cq_ref[rows, :] + pltpu.roll(q, RET_HD // 2, 1) * sq_ref[rows, :]
            kr = k * ck_ref[rows, :] + pltpu.roll(k, RET_HD // 2, 1) * sk_ref[rows, :]
            scores = lax.dot_general(qr.astype(BF16), kr.astype(BF16), (((1,), (1,)), ((), ())),
                                     preferred_element_type=F32)
            p = (scores * dec_ref[m]).astype(BF16)
            lhs = jnp.concatenate([p, (qr * xz_ref[0, m]).astype(BF16)], axis=1)
            rhs = jnp.concatenate([vb, state.astype(BF16)], axis=0)
            o = jnp.dot(lhs, rhs, preferred_element_type=F32)
            kz = (kr * xz_ref[1, m]).astype(BF16)
            kv = lax.dot_general(kz, vb, (((0,), (0,)), ((), ())), preferred_element_type=F32)
            state = state * gc + kv
            mu = jnp.mean(o, axis=-1, keepdims=True)
            d = o - mu
            var = jnp.mean(d * d, axis=-1, keepdims=True)
            on = (d * lax.rsqrt(var + EPS)) * gnw
            y_ref[N_LRU_LB + m, rows, :] = (_silu(g) * on).astype(BF16)
        state_ref[m] = state

    def projection(wr_lru, wr_rest):
        res = jnp.dot(u_ref[...], w_ref[...], preferred_element_type=F32)
        for kk in range(MIX_LB):
            lb = j * MIX_LB + kk
            blk = res[:, kk * LANES:(kk + 1) * LANES]
            wr_lru[jnp.minimum(lb, N_LRU_LB), pl.ds(SUBLANES, MIX_TS), :] = blk
            wr_rest[jnp.where(lb >= N_LRU_LB, lb - N_LRU_LB, N_REST_LB), :, :] = blk.astype(BF16)

    first, last, even = i == 0, i == N_BLK, lax.rem(i, 2) == 0
    mid = jnp.logical_not(jnp.logical_or(first, last))

    @pl.when(first)
    def _():
        projection(plru_a, prest_a)

    @pl.when(jnp.logical_and(mid, even))
    def _():
        head_groups(plru_b, prest_b, plru_a)
        projection(plru_a, prest_a)

    @pl.when(jnp.logical_and(mid, jnp.logical_not(even)))
    def _():
        head_groups(plru_a, prest_a, plru_b)
        projection(plru_b, prest_b)

    @pl.when(last)
    def _():
        head_groups(plru_b, prest_b, plru_a)


def _mix(x, ln, w_slabs, unit_tab, wcat, rot, base, decay, xz, w_gate, w_up, w_down, w_out):
    assert N_BLK % 2 == 0
    prev_blk = lambda i, j: (jnp.maximum(i - 1, 0), 0, 0, 0)
    const = lambda a: pl.BlockSpec(a.shape, lambda i, j: (0,) * a.ndim, pipeline_mode=pl.Buffered(1))
    chunk = lambda i, j, n: jnp.minimum(i * MIX_J + j, n - 1)
    nt = LRU_R // SUBLANES
    n_steps = N_BLK * MIX_J
    up_rows = D_MODEL // n_steps
    down_rows = 4 * up_rows
    n_up, n_down, n_ff = D_MODEL // up_rows, D_FF // down_rows, D_FF // FFN_TF
    up_in = pl.BlockSpec((up_rows, D_FF), lambda i, j: (chunk(i, j, n_up), 0))
    up_out = pl.BlockSpec((n_ff, up_rows, 2 * FFN_TF), lambda i, j: (0, chunk(i, j, n_up), 0))
    down_io = pl.BlockSpec((down_rows, D_MODEL), lambda i, j: (chunk(i, j, n_down), 0))
    out_io = pl.BlockSpec((up_rows, D_MODEL), lambda i, j: (chunk(i, j, n_up), 0))
    lru_buf = pltpu.VMEM((N_LRU_LB + 1, SUBLANES + MIX_TS, LANES), F32)
    rest_buf = pltpu.VMEM((N_REST_LB + 1, MIX_TS, LANES), BF16)
    return pl.pallas_call(
        _mix_body,
        grid=(N_BLK + 1, MIX_J),
        in_specs=[
            pl.BlockSpec(memory_space=pl.ANY),
            pl.BlockSpec((1, D_MODEL), lambda i, j: (0, 0)),
            pl.BlockSpec((D_MODEL, MIX_NW), lambda i, j: (0, j)),
            const(unit_tab), const(wcat), const(rot), const(base), const(decay), const(xz),
            up_in, up_in, down_io, out_io,
        ],
        out_specs=[pl.BlockSpec((None, 2 * N_LRU_LB, MIX_TS, LANES), prev_blk), up_out, down_io, out_io],
        out_shape=[
            jax.ShapeDtypeStruct((N_BLK, 2 * N_LRU_LB, MIX_TS, LANES), BF16),
            jax.ShapeDtypeStruct((n_ff, D_MODEL, 2 * FFN_TF), BF16),
            jax.ShapeDtypeStruct(w_down.shape, BF16),
            jax.ShapeDtypeStruct(w_out.shape, BF16),
        ],
        scratch_shapes=[
            pltpu.VMEM((MIX_TS, D_MODEL), BF16),
            lru_buf, lru_buf, rest_buf, rest_buf,
            pltpu.VMEM((UNITS * LRU_SUB, nt, 2 * SUBLANES, LANES), F32),
            pltpu.VMEM((UNITS * LRU_SUB, nt, 2 * SUBLANES, LANES), F32),
            pltpu.VMEM((N_LRU_LB, SUBLANES, LANES), F32),
            pltpu.VMEM((RET_HEADS, RET_HD, RET_HD), F32),
            pltpu.VMEM((MIX_TS, RET_HD), F32),
            pltpu.VMEM((MIX_TS, RET_HD), F32),
            pltpu.VMEM((MIX_TS, RET_HD), F32),
            pltpu.VMEM((MIX_TS, RET_HD), F32),
            pltpu.VMEM((MIX_TS, D_MODEL), F32),
            pltpu.SemaphoreType.DMA(()),
        ],
        compiler_params=_params("arbitrary", "arbitrary"),
        name="mix",
    )(x, ln, w_slabs, unit_tab, wcat, rot, base, decay, xz, w_gate, w_up, w_down, w_out)


def _outproj_body(x_ref, y_ref, wo_ref, o_ref):
    for rb in range(OUT_TS // MIX_TS):
        rows = slice(rb * MIX_TS, (rb + 1) * MIX_TS)
        y = jnp.concatenate([y_ref[rb, lb] for lb in range(2 * N_LRU_LB)], axis=1)
        o_ref[rows, :] = x_ref[rows, :] + jnp.dot(y, wo_ref[...], preferred_element_type=F32)


def _outproj(x, y, w_out):
    return pl.pallas_call(
        _outproj_body,
        grid=(SEQ // OUT_TS,),
        in_specs=[
            pl.BlockSpec((OUT_TS, D_MODEL), lambda i: (i, 0)),
            pl.BlockSpec((OUT_TS // MIX_TS, 2 * N_LRU_LB, MIX_TS, LANES), lambda i: (i, 0, 0, 0)),
            pl.BlockSpec((D_MODEL, D_MODEL), lambda i: (0, 0)),
        ],
        out_specs=pl.BlockSpec((OUT_TS, D_MODEL), lambda i: (i, 0)),
        out_shape=jax.ShapeDtypeStruct((SEQ, D_MODEL), F32),
        compiler_params=_params("parallel"),
        name="outproj",
    )(x, y, w_out)


def _ffn_body(h_hbm, ln2_ref, wgu_ref, wd_ref, fnw_ref, o_ref, u_ref, hbuf_ref, hsem):
    i = pl.program_id(0)
    f = pl.program_id(1)

    def h_copy(blk):
        return pltpu.make_async_copy(h_hbm.at[pl.ds(blk * FFN_TS, FFN_TS), :], hbuf_ref, hsem)

    @pl.when(jnp.logical_and(i == 0, f == 0))
    def _():
        h_copy(0).start()

    @pl.when(f == 0)
    def _():
        h_copy(i).wait()

        def step(c, carry):
            r0 = pl.multiple_of(c * NORM_ROWS, NORM_ROWS)
            hrows = hbuf_ref[pl.ds(r0, NORM_ROWS), :]
            u_ref[pl.ds(r0, NORM_ROWS), :] = _rmsnorm_rows(hrows, ln2_ref[...]).astype(BF16)
            o_ref[pl.ds(r0, NORM_ROWS), :] = hrows
            return carry

        lax.fori_loop(0, FFN_TS // NORM_ROWS, step, 0)

    @pl.when(jnp.logical_and(f == 1, i + 1 < pl.num_programs(0)))
    def _():
        h_copy(i + 1).start()

    for half in range(FFN_TS // FFN_HALF):
        rows = slice(half * FFN_HALF, (half + 1) * FFN_HALF)
        gu = jnp.dot(u_ref[rows, :], wgu_ref[...], preferred_element_type=F32)
        gate, up = gu[:, :FFN_TF], gu[:, FFN_TF:]
        act = (_silu(gate) * up).astype(BF16)
        o_ref[rows, :] += jnp.dot(act, wd_ref[...], preferred_element_type=F32)

    @pl.when(f == pl.num_programs(1) - 1)
    def _():
        def step(c, carry):
            r0 = pl.multiple_of(c * NORM_ROWS, NORM_ROWS)
            rows = pl.ds(r0, NORM_ROWS)
            o_ref[rows, :] = _rmsnorm_rows(o_ref[rows, :], fnw_ref[...])
            return carry

        lax.fori_loop(0, FFN_TS // NORM_ROWS, step, 0)


def _ffn(h1, ln2, wgu, wd, fnw):
    return pl.pallas_call(
        _ffn_body,
        grid=(SEQ // FFN_TS, D_FF // FFN_TF),
        in_specs=[
            pl.BlockSpec(memory_space=pl.ANY),
            pl.BlockSpec((1, D_MODEL), lambda i, f: (0, 0)),
            pl.BlockSpec((None, D_MODEL, 2 * FFN_TF), lambda i, f: (f, 0, 0)),
            pl.BlockSpec((FFN_TF, D_MODEL), lambda i, f: (f, 0)),
            pl.BlockSpec((1, D_MODEL), lambda i, f: (0, 0)),
        ],
        out_specs=pl.BlockSpec((FFN_TS, D_MODEL), lambda i, f: (i, 0)),
        out_shape=jax.ShapeDtypeStruct((SEQ, D_MODEL), F32),
        scratch_shapes=[
            pltpu.VMEM((FFN_TS, D_MODEL), BF16),
            pltpu.VMEM((FFN_TS, D_MODEL), F32),
            pltpu.SemaphoreType.DMA(()),
        ],
        compiler_params=_params("arbitrary", "arbitrary"),
        name="ffn",
    )(h1, ln2, wgu, wd, fnw)


def _retention_tables():
    H, Dh, C = RET_HEADS, RET_HD, RET_C
    inv_freq = ROPE_BASE ** (-np.arange(0, Dh, 2, dtype=np.float64) / Dh)
    inv2 = np.concatenate([inv_freq, inv_freq])
    off = np.arange(MIX_TS, dtype=np.float64)[:, None] * inv2[None, :]
    base = (np.arange(N_BLK, dtype=np.float64) * MIX_TS)[:, None, None] * inv2[None, None, :]
    log_gamma = np.log1p(-np.exp2(-5.0 - np.arange(H, dtype=np.float64)))
    idx = np.arange(C)
    diff = idx[:, None] - idx[None, :]
    decay = np.where(diff >= 0, np.exp(log_gamma[:, None, None] * np.maximum(diff, 0)[None]), 0.0)
    zeta = np.exp(log_gamma[:, None] * (C - 1 - idx)[None, :])
    xi = np.exp(log_gamma[:, None] * (idx + 1)[None, :])
    gc = np.exp(log_gamma * C)
    per_head = lambda t: np.broadcast_to(t[:, :, None], (H, C, Dh))
    tables = (np.stack([np.cos(off), np.sin(off)]),
              np.concatenate([np.cos(base), np.sin(base)], axis=1),
              decay,
              np.stack([per_head(xi), per_head(zeta)]),
              np.broadcast_to(gc[:, None, None], (H, 1, Dh)))
    return tuple(jnp.asarray(np.ascontiguousarray(t, dtype=np.float32)) for t in tables)


def kernel(x, ln1_w, w_in, conv_w, conv_b, gate_a_w, gate_a_b, gate_x_w, gate_x_b, lru_lambda, ret_gn_w,
           w_out, ln2_w, w_ffn_gate, w_ffn_up, w_ffn_down, final_norm_w):
    x2 = x.reshape(SEQ, D_MODEL)
    row = lambda v: v.reshape(1, -1)
    by_lb = lambda v: v.reshape(-1, 1, LANES)
    wcat = (0.5 * jnp.concatenate([gate_a_w[0], gate_x_w[0]], axis=-1)).astype(BF16)
    cw = conv_w[0].reshape(CONV_W, N_LRU_LB, LANES).transpose(1, 0, 2)
    rot, base, decay, xz, gc = _retention_tables()
    unit_tab = jnp.concatenate(
        [cw, by_lb(conv_b[0]), by_lb(gate_a_b[0]), by_lb(gate_x_b[0]), by_lb(lru_lambda[0]), gc,
         by_lb(ret_gn_w[0]), jnp.zeros((N_LRU_LB, UNIT_ROWS - U_GN_W - 1, LANES), F32)], axis=1)
    y, wgu, wd, wo = _mix(x2, row(ln1_w[0]), w_in[0].astype(BF16), unit_tab, wcat, rot, base, decay, xz,
                          w_ffn_gate[0], w_ffn_up[0], w_ffn_down[0], w_out[0])
    h1 = _outproj(x2, y, wo)
    out = _ffn(h1, row(ln2_w[0]), wgu, wd, row(final_norm_w))
    return out.reshape(1, SEQ, D_MODEL)
```

```python
import jax
import jax.numpy as jnp
import numpy as np
from jax import lax
from jax.experimental import pallas as pl
from jax.experimental.pallas import tpu as pltpu

D_MODEL = 2048
SEQ = 8192
D_LRU = 1024
D_RET = 1024
CONV_W = 4
RET_HEADS = 8
RET_HD = 128
ROPE_BASE = 10000.0
D_FF = 5632
D_IN = 6144
EPS = 1e-6
GELU_C = 0.7978845608028654

SUBLANES = 8
LANES = 128
VMEM_LIMIT_BYTES = 56 * 1024 * 1024

F32 = jnp.float32
BF16 = jnp.bfloat16

NORM_ROWS = 128
MIX_TS = 512
MIX_J = 4
MIX_NW = D_IN // MIX_J
MIX_LB = MIX_NW // LANES
N_BLK = SEQ // MIX_TS
N_LRU_LB = D_LRU // LANES
UNITS = N_LRU_LB // MIX_J
N_REST_LB = (D_IN - D_LRU) // LANES
LRU_R = 128
LRU_SUB = MIX_TS // LRU_R
RET_C = 256
OUT_TS = 1024
FFN_TS, FFN_TF = 1024, 512
FFN_HALF = 512
U_CONV_B, U_GATE_A_B, U_GATE_X_B, U_LAMBDA, U_CHUNK_DECAY, U_GN_W = 4, 5, 6, 7, 8, 9
UNIT_ROWS = 16


def _rmsnorm_rows(x, w):
    ms = jnp.mean(x * x, axis=-1, keepdims=True)
    return (x * lax.rsqrt(ms + EPS)) * w


def _silu(x):
    h = 0.5 * x
    return h * jnp.tanh(h) + h


def _params(*sem):
    return pltpu.CompilerParams(dimension_semantics=sem, vmem_limit_bytes=VMEM_LIMIT_BYTES)


def _lru_sub_block(r0, hc, slot, m, rd_lru, rd_rest, y_ref, sa_ref, sb_ref, cw, cb, wcat, ba, bx, lsl4):
    nt = LRU_R // SUBLANES
    base = r0 + SUBLANES
    xc = cb + rd_lru[m, pl.ds(base - 3, LRU_R), :] * cw[0:1]
    xc = xc + rd_lru[m, pl.ds(base - 2, LRU_R), :] * cw[1:2]
    xc = xc + rd_lru[m, pl.ds(base - 1, LRU_R), :] * cw[2:3]
    xc = xc + rd_lru[m, pl.ds(base, LRU_R), :] * cw[3:4]
    gates = jnp.dot(xc.astype(BF16), wcat, preferred_element_type=F32)
    tr = jnp.tanh(gates[:, :LANES] + ba)
    ti = jnp.tanh(gates[:, LANES:] + bx)
    log_a = (tr + 1.0) * lsl4
    a = jnp.exp(log_a)
    v = -jnp.tanh(log_a)
    coef = jnp.where(v > 0.0, v * lax.rsqrt((v + v) * (1.0 + v)), 0.0)
    b = coef * ((ti + 1.0) * xc)
    a3 = a.reshape(nt, SUBLANES, LANES)
    b3 = b.reshape(nt, SUBLANES, LANES)
    for s in (1, 2, 4):
        sa_ref[slot, :, SUBLANES:2 * SUBLANES, :] = a3
        sb_ref[slot, :, SUBLANES:2 * SUBLANES, :] = b3
        a_sh = sa_ref[slot, :, SUBLANES - s:2 * SUBLANES - s, :]
        b_sh = sb_ref[slot, :, SUBLANES - s:2 * SUBLANES - s, :]
        b3 = a3 * b_sh + b3
        a3 = a3 * a_sh
    tiles = []
    for t in range(nt):
        ht = a3[t] * hc + b3[t]
        hc = ht[SUBLANES - 1:SUBLANES, :]
        tiles.append(ht)
    h = jnp.concatenate(tiles, axis=0)
    g = rd_rest[m, pl.ds(r0, LRU_R), :].astype(F32)
    tg = jnp.tanh(g * (GELU_C + (GELU_C * 0.044715) * (g * g)))
    hg = 0.5 * g
    y_ref[m, pl.ds(r0, LRU_R), :] = (h * (hg * tg + hg)).astype(BF16)
    return hc


def _mix_body(x_hbm, ln_ref, wx_ref, wlg_ref, wq_ref, wk_ref, wv_ref, wrg_ref,
              unit_ref, wg_ref, rot_ref, base_ref, dec_ref, xz_ref,
              fg_ref, fu_ref, fd_ref, fo_ref,
              y_ref, bgu_ref, bd_ref, bo_ref,
              u_ref, plru_a, plru_b, prest_a, prest_b, sa_ref, sb_ref, h_ref, state_ref,
              cq_ref, sq_ref, ck_ref, sk_ref, xbuf_ref, xsem):
    i = pl.program_id(0)
    j = pl.program_id(1)
    nt = LRU_R // SUBLANES

    for f in range(D_FF // FFN_TF):
        bgu_ref[f, :, 0:FFN_TF] = fg_ref[:, f * FFN_TF:(f + 1) * FFN_TF].astype(BF16)
        bgu_ref[f, :, FFN_TF:2 * FFN_TF] = fu_ref[:, f * FFN_TF:(f + 1) * FFN_TF].astype(BF16)
    bd_ref[...] = fd_ref[...].astype(BF16)
    bo_ref[...] = fo_ref[...].astype(BF16)

    @pl.when(jnp.logical_and(i == 0, j == 0))
    def _():
        h_ref[...] = jnp.zeros_like(h_ref)
        state_ref[...] = jnp.zeros_like(state_ref)
        plru_a[:, 0:SUBLANES, :] = jnp.zeros((N_LRU_LB + 1, SUBLANES, LANES), F32)

    def x_copy(blk):
        return pltpu.make_async_copy(x_hbm.at[pl.ds(blk * MIX_TS, MIX_TS), :], xbuf_ref, xsem)

    @pl.when(jnp.logical_and(i == 0, j == 0))
    def _():
        x_copy(0).start()

    @pl.when(jnp.logical_and(j == 0, i < N_BLK))
    def _():
        x_copy(i).wait()

        def step(r, carry):
            r0 = pl.multiple_of(r * NORM_ROWS, NORM_ROWS)
            x = xbuf_ref[pl.ds(r0, NORM_ROWS), :]
            u_ref[pl.ds(r0, NORM_ROWS), :] = _rmsnorm_rows(x, ln_ref[...]).astype(BF16)
            return carry

        lax.fori_loop(0, MIX_TS // NORM_ROWS, step, 0)

    @pl.when(jnp.logical_and(j == 1, i + 1 < N_BLK))
    def _():
        x_copy(i + 1).start()

    @pl.when(j == 0)
    def _():
        blk = jnp.maximum(i - 1, 0)
        cb_, sb_ = base_ref[blk][0:1], base_ref[blk][1:2]
        cos = cb_ * rot_ref[0] - sb_ * rot_ref[1]
        sin = sb_ * rot_ref[0] + cb_ * rot_ref[1]
        lane = lax.broadcasted_iota(jnp.int32, (1, RET_HD), 1)
        sin = jnp.where(lane < RET_HD // 2, -sin, sin)
        cq_ref[...] = cos
        sq_ref[...] = sin
        ck_ref[...] = cos * (RET_HD ** -0.5)
        sk_ref[...] = sin * (RET_HD ** -0.5)

    def head_groups(rd_lru, rd_rest, wr_lru):
        sa_ref[:, :, 0:SUBLANES, :] = jnp.ones((UNITS * LRU_SUB, nt, SUBLANES, LANES), F32)
        sb_ref[:, :, 0:SUBLANES, :] = jnp.zeros((UNITS * LRU_SUB, nt, SUBLANES, LANES), F32)
        for uu in range(UNITS):
            head_group(j * UNITS + uu, uu * LRU_SUB, rd_lru, rd_rest, wr_lru)

    def head_group(m, slot0, rd_lru, rd_rest, wr_lru):
        unit = unit_ref[m]
        cw, cb = unit[0:CONV_W], unit[U_CONV_B:U_CONV_B + 1]
        ba, bx = 0.5 * unit[U_GATE_A_B:U_GATE_A_B + 1], 0.5 * unit[U_GATE_X_B:U_GATE_X_B + 1]
        lam = unit[U_LAMBDA:U_LAMBDA + 1]
        gc, gnw = unit[U_CHUNK_DECAY:U_CHUNK_DECAY + 1], unit[U_GN_W:U_GN_W + 1]
        lsl4 = -4.0 * (jnp.maximum(-lam, 0.0) + jnp.log1p(jnp.exp(-jnp.abs(lam))))
        hc = h_ref[m][0:1, :]
        for slot in range(LRU_SUB):
            hc = _lru_sub_block(slot * LRU_R, hc, slot0 + slot, m, rd_lru, rd_rest, y_ref, sa_ref, sb_ref,
                                cw, cb, wg_ref[m], ba, bx, lsl4)
        h_ref[m] = jnp.broadcast_to(hc, (SUBLANES, LANES))
        wr_lru[m, 0:SUBLANES, :] = rd_lru[m, pl.ds(MIX_TS, SUBLANES), :]

        state = state_ref[m]
        for c in range(MIX_TS // RET_C):
            rows = slice(c * RET_C, (c + 1) * RET_C)
            q = rd_rest[8 + m, rows, :].astype(F32)
            k = rd_rest[16 + m, rows, :].astype(F32)
            vb = rd_rest[24 + m, rows, :]
            g = rd_rest[32 + m, rows, :].astype(F32)
            qr = q * cq_ref[rows, :] + pltpu.roll(q, RET_HD // 2, 1) * sq_ref[rows, :]
            kr = k * ck_ref[rows, :] + pltpu.roll(k, RET_HD // 2, 1) * sk_ref[rows, :]
            scores = lax.dot_general(qr.astype(BF16), kr.astype(BF16), (((1,), (1,)), ((), ())),
                                     preferred_element_type=F32)
            p = (scores * dec_ref[m]).astype(BF16)
            lhs = jnp.concatenate([p, (qr * xz_ref[0, m]).astype(BF16)], axis=1)
            rhs = jnp.concatenate([vb, state.astype(BF16)], axis=0)
            o = jnp.dot(lhs, rhs, preferred_element_type=F32)
            kz = (kr * xz_ref[1, m]).astype(BF16)
            kv = lax.dot_general(kz, vb, (((0,), (0,)), ((), ())), preferred_element_type=F32)
            state = state * gc + kv
            mu = jnp.mean(o, axis=-1, keepdims=True)
            d = o - mu
            var = jnp.mean(d * d, axis=-1, keepdims=True)
            on = (d * lax.rsqrt(var + EPS)) * gnw
            y_ref[N_LRU_LB + m, rows, :] = (_silu(g) * on).astype(BF16)
        state_ref[m] = state

    def projection(wr_lru, wr_rest):
        w = jnp.concatenate([r[...] for r in (wx_ref, wlg_ref, wq_ref, wk_ref, wv_ref, wrg_ref)], axis=1)
        res = jnp.dot(u_ref[...], w, preferred_element_type=F32)
        for kk in range(MIX_LB):
            lb = (kk // UNITS) * N_LRU_LB + j * UNITS + kk % UNITS
            blk = res[:, kk * LANES:(kk + 1) * LANES]
            wr_lru[jnp.minimum(lb, N_LRU_LB), pl.ds(SUBLANES, MIX_TS), :] = blk
            wr_rest[jnp.where(lb >= N_LRU_LB, lb - N_LRU_LB, N_REST_LB), :, :] = blk.astype(BF16)

    first, last, even = i == 0, i == N_BLK, lax.rem(i, 2) == 0
    mid = jnp.logical_not(jnp.logical_or(first, last))

    @pl.when(first)
    def _():
        projection(plru_a, prest_a)

    @pl.when(jnp.logical_and(mid, even))
    def _():
        head_groups(plru_b, prest_b, plru_a)
        projection(plru_a, prest_a)

    @pl.when(jnp.logical_and(mid, jnp.logical_not(even)))
    def _():
        head_groups(plru_a, prest_a, plru_b)
        projection(plru_b, prest_b)

    @pl.when(last)
    def _():
        head_groups(plru_b, prest_b, plru_a)


def _mix(x, ln, w_slabs, unit_tab, wcat, rot, base, decay, xz, w_gate, w_up, w_down, w_out):
    assert N_BLK % 2 == 0
    prev_blk = lambda i, j: (jnp.maximum(i - 1, 0), 0, 0, 0)
    const = lambda a: pl.BlockSpec(a.shape, lambda i, j: (0,) * a.ndim, pipeline_mode=pl.Buffered(1))
    chunk = lambda i, j, n: jnp.minimum(i * MIX_J + j, n - 1)
    nt = LRU_R // SUBLANES
    n_steps = N_BLK * MIX_J
    up_rows = D_MODEL // n_steps
    down_rows = 4 * up_rows
    n_up, n_down, n_ff = D_MODEL // up_rows, D_FF // down_rows, D_FF // FFN_TF
    up_in = pl.BlockSpec((up_rows, D_FF), lambda i, j: (chunk(i, j, n_up), 0))
    up_out = pl.BlockSpec((n_ff, up_rows, 2 * FFN_TF), lambda i, j: (0, chunk(i, j, n_up), 0))
    down_io = pl.BlockSpec((down_rows, D_MODEL), lambda i, j: (chunk(i, j, n_down), 0))
    out_io = pl.BlockSpec((up_rows, D_MODEL), lambda i, j: (chunk(i, j, n_up), 0))
    lru_buf = pltpu.VMEM((N_LRU_LB + 1, SUBLANES + MIX_TS, LANES), F32)
    rest_buf = pltpu.VMEM((N_REST_LB + 1, MIX_TS, LANES), BF16)
    return pl.pallas_call(
        _mix_body,
        grid=(N_BLK + 1, MIX_J),
        in_specs=[
            pl.BlockSpec(memory_space=pl.ANY),
            pl.BlockSpec((1, D_MODEL), lambda i, j: (0, 0)),
            *[pl.BlockSpec((D_MODEL, UNITS * LANES), lambda i, j, c=c: (0, c * MIX_J + j))
              for c in range(D_IN // D_LRU)],
            const(unit_tab), const(wcat), const(rot), const(base), const(decay), const(xz),
            up_in, up_in, down_io, out_io,
        ],
        out_specs=[pl.BlockSpec((None, 2 * N_LRU_LB, MIX_TS, LANES), prev_blk), up_out, down_io, out_io],
        out_shape=[
            jax.ShapeDtypeStruct((N_BLK, 2 * N_LRU_LB, MIX_TS, LANES), BF16),
            jax.ShapeDtypeStruct((n_ff, D_MODEL, 2 * FFN_TF), BF16),
            jax.ShapeDtypeStruct(w_down.shape, BF16),
            jax.ShapeDtypeStruct(w_out.shape, BF16),
        ],
        scratch_shapes=[
            pltpu.VMEM((MIX_TS, D_MODEL), BF16),
            lru_buf, lru_buf, rest_buf, rest_buf,
            pltpu.VMEM((UNITS * LRU_SUB, nt, 2 * SUBLANES, LANES), F32),
            pltpu.VMEM((UNITS * LRU_SUB, nt, 2 * SUBLANES, LANES), F32),
            pltpu.VMEM((N_LRU_LB, SUBLANES, LANES), F32),
            pltpu.VMEM((RET_HEADS, RET_HD, RET_HD), F32),
            pltpu.VMEM((MIX_TS, RET_HD), F32),
            pltpu.VMEM((MIX_TS, RET_HD), F32),
            pltpu.VMEM((MIX_TS, RET_HD), F32),
            pltpu.VMEM((MIX_TS, RET_HD), F32),
            pltpu.VMEM((MIX_TS, D_MODEL), F32),
            pltpu.SemaphoreType.DMA(()),
        ],
        compiler_params=_params("arbitrary", "arbitrary"),
        name="mix",
    )(x, ln, *[w_slabs] * (D_IN // D_LRU), unit_tab, wcat, rot, base, decay, xz, w_gate, w_up, w_down, w_out)


def _outproj_body(x_ref, y_ref, wo_ref, o_ref):
    for rb in range(OUT_TS // MIX_TS):
        rows = slice(rb * MIX_TS, (rb + 1) * MIX_TS)
        y = jnp.concatenate([y_ref[rb, lb] for lb in range(2 * N_LRU_LB)], axis=1)
        o_ref[rows, :] = x_ref[rows, :] + jnp.dot(y, wo_ref[...], preferred_element_type=F32)


def _outproj(x, y, w_out):
    return pl.pallas_call(
        _outproj_body,
        grid=(SEQ // OUT_TS,),
        in_specs=[
            pl.BlockSpec((OUT_TS, D_MODEL), lambda i: (i, 0)),
            pl.BlockSpec((OUT_TS // MIX_TS, 2 * N_LRU_LB, MIX_TS, LANES), lambda i: (i, 0, 0, 0)),
            pl.BlockSpec((D_MODEL, D_MODEL), lambda i: (0, 0)),
        ],
        out_specs=pl.BlockSpec((OUT_TS, D_MODEL), lambda i: (i, 0)),
        out_shape=jax.ShapeDtypeStruct((SEQ, D_MODEL), F32),
        compiler_params=_params("parallel"),
        name="outproj",
    )(x, y, w_out)


def _ffn_body(h_hbm, ln2_ref, wgu_ref, wd_ref, fnw_ref, o_ref, u_ref, hbuf_ref, hsem):
    i = pl.program_id(0)
    f = pl.program_id(1)

    def h_copy(blk):
        return pltpu.make_async_copy(h_hbm.at[pl.ds(blk * FFN_TS, FFN_TS), :], hbuf_ref, hsem)

    @pl.when(jnp.logical_and(i == 0, f == 0))
    def _():
        h_copy(0).start()

    @pl.when(f == 0)
    def _():
        h_copy(i).wait()

        def step(c, carry):
            r0 = pl.multiple_of(c * NORM_ROWS, NORM_ROWS)
            hrows = hbuf_ref[pl.ds(r0, NORM_ROWS), :]
            u_ref[pl.ds(r0, NORM_ROWS), :] = _rmsnorm_rows(hrows, ln2_ref[...]).astype(BF16)
            o_ref[pl.ds(r0, NORM_ROWS), :] = hrows
            return carry

        lax.fori_loop(0, FFN_TS // NORM_ROWS, step, 0)

    @pl.when(jnp.logical_and(f == 1, i + 1 < pl.num_programs(0)))
    def _():
        h_copy(i + 1).start()

    for half in range(FFN_TS // FFN_HALF):
        rows = slice(half * FFN_HALF, (half + 1) * FFN_HALF)
        gu = jnp.dot(u_ref[rows, :], wgu_ref[...], preferred_element_type=F32)
        gate, up = gu[:, :FFN_TF], gu[:, FFN_TF:]
        act = (_silu(gate) * up).astype(BF16)
        o_ref[rows, :] += jnp.dot(act, wd_ref[...], preferred_element_type=F32)

    @pl.when(f == pl.num_programs(1) - 1)
    def _():
        def step(c, carry):
            r0 = pl.multiple_of(c * NORM_ROWS, NORM_ROWS)
            rows = pl.ds(r0, NORM_ROWS)
            o_ref[rows, :] = _rmsnorm_rows(o_ref[rows, :], fnw_ref[...])
            return carry

        lax.fori_loop(0, FFN_TS // NORM_ROWS, step, 0)


def _ffn(h1, ln2, wgu, wd, fnw):
    return pl.pallas_call(
        _ffn_body,
        grid=(SEQ // FFN_TS, D_FF // FFN_TF),
        in_specs=[
            pl.BlockSpec(memory_space=pl.ANY),
            pl.BlockSpec((1, D_MODEL), lambda i, f: (0, 0)),
            pl.BlockSpec((None, D_MODEL, 2 * FFN_TF), lambda i, f: (f, 0, 0)),
            pl.BlockSpec((FFN_TF, D_MODEL), lambda i, f: (f, 0)),
            pl.BlockSpec((1, D_MODEL), lambda i, f: (0, 0)),
        ],
        out_specs=pl.BlockSpec((FFN_TS, D_MODEL), lambda i, f: (i, 0)),
        out_shape=jax.ShapeDtypeStruct((SEQ, D_MODEL), F32),
        scratch_shapes=[
            pltpu.VMEM((FFN_TS, D_MODEL), BF16),
            pltpu.VMEM((FFN_TS, D_MODEL), F32),
            pltpu.SemaphoreType.DMA(()),
        ],
        compiler_params=_params("arbitrary", "arbitrary"),
        name="ffn",
    )(h1, ln2, wgu, wd, fnw)


def _retention_tables():
    H, Dh, C = RET_HEADS, RET_HD, RET_C
    inv_freq = ROPE_BASE ** (-np.arange(0, Dh, 2, dtype=np.float64) / Dh)
    inv2 = np.concatenate([inv_freq, inv_freq])
    off = np.arange(MIX_TS, dtype=np.float64)[:, None] * inv2[None, :]
    base = (np.arange(N_BLK, dtype=np.float64) * MIX_TS)[:, None, None] * inv2[None, None, :]
    log_gamma = np.log1p(-np.exp2(-5.0 - np.arange(H, dtype=np.float64)))
    idx = np.arange(C)
    diff = idx[:, None] - idx[None, :]
    decay = np.where(diff >= 0, np.exp(log_gamma[:, None, None] * np.maximum(diff, 0)[None]), 0.0)
    zeta = np.exp(log_gamma[:, None] * (C - 1 - idx)[None, :])
    xi = np.exp(log_gamma[:, None] * (idx + 1)[None, :])
    gc = np.exp(log_gamma * C)
    per_head = lambda t: np.broadcast_to(t[:, :, None], (H, C, Dh))
    tables = (np.stack([np.cos(off), np.sin(off)]),
              np.concatenate([np.cos(base), np.sin(base)], axis=1),
              decay,
              np.stack([per_head(xi), per_head(zeta)]),
              np.broadcast_to(gc[:, None, None], (H, 1, Dh)))
    return tuple(jnp.asarray(np.ascontiguousarray(t, dtype=np.float32)) for t in tables)


def kernel(x, ln1_w, w_in, conv_w, conv_b, gate_a_w, gate_a_b, gate_x_w, gate_x_b, lru_lambda, ret_gn_w,
           w_out, ln2_w, w_ffn_gate, w_ffn_up, w_ffn_down, final_norm_w):
    x2 = x.reshape(SEQ, D_MODEL)
    row = lambda v: v.reshape(1, -1)
    by_lb = lambda v: v.reshape(-1, 1, LANES)
    wcat = (0.5 * jnp.concatenate([gate_a_w[0], gate_x_w[0]], axis=-1)).astype(BF16)
    cw = conv_w[0].reshape(CONV_W, N_LRU_LB, LANES).transpose(1, 0, 2)
    rot, base, decay, xz, gc = _retention_tables()
    unit_tab = jnp.concatenate(
        [cw, by_lb(conv_b[0]), by_lb(gate_a_b[0]), by_lb(gate_x_b[0]), by_lb(lru_lambda[0]), gc,
         by_lb(ret_gn_w[0]), jnp.zeros((N_LRU_LB, UNIT_ROWS - U_GN_W - 1, LANES), F32)], axis=1)
    y, wgu, wd, wo = _mix(x2, row(ln1_w[0]), w_in[0].astype(BF16), unit_tab, wcat, rot, base, decay, xz,
                          w_ffn_gate[0], w_ffn_up[0], w_ffn_down[0], w_out[0])
    h1 = _outproj(x2, y, wo)
    out = _ffn(h1, row(ln2_w[0]), wgu, wd, row(final_norm_w))
    return out.reshape(1, SEQ, D_MODEL)
```

```python
import jax
import jax.numpy as jnp
import numpy as np
from jax import lax
from jax.experimental import pallas as pl
from jax.experimental.pallas import tpu as pltpu

D_MODEL = 2048
SEQ = 8192
D_LRU = 1024
D_RET = 1024
CONV_W = 4
RET_HEADS = 8
RET_HD = 128
ROPE_BASE = 10000.0
D_FF = 5632
D_IN = 6144
EPS = 1e-6
GELU_C = 0.7978845608028654

SUBLANES = 8
LANES = 128
VMEM_LIMIT_BYTES = 56 * 1024 * 1024

F32 = jnp.float32
BF16 = jnp.bfloat16

NORM_ROWS = 128
MIX_TS = 512
MIX_J = 4
MIX_NW = D_IN // MIX_J
MIX_LB = MIX_NW // LANES
N_BLK = SEQ // MIX_TS
N_LRU_LB = D_LRU // LANES
UNITS = N_LRU_LB // MIX_J
N_REST_LB = (D_IN - D_LRU) // LANES
C_LRU_X, C_LRU_GATE, C_Q, C_K, C_V, C_RET_GATE = range(6)
LRU_R = 128
LRU_SUB = MIX_TS // LRU_R
RET_C = 256
OUT_TS = 1024
FFN_TS, FFN_TF = 1024, 512
FFN_HALF = 512
U_CONV_B, U_GATE_A_B, U_GATE_X_B, U_LAMBDA, U_CHUNK_DECAY, U_GN_W = 4, 5, 6, 7, 8, 9
UNIT_ROWS = 16


def _rmsnorm_rows(x, w):
    ms = jnp.mean(x * x, axis=-1, keepdims=True)
    return (x * lax.rsqrt(ms + EPS)) * w


def _silu(x):
    h = 0.5 * x
    return h * jnp.tanh(h) + h


def _params(*sem):
    return pltpu.CompilerParams(dimension_semantics=sem, vmem_limit_bytes=VMEM_LIMIT_BYTES)


def _lru_sub_block(r0, hc, slot, m, rd_lru, rd_rest, y_ref, sa_ref, sb_ref, cw, cb, wcat, ba, bx, lsl4):
    nt = LRU_R // SUBLANES
    base = r0 + SUBLANES
    xc = cb + rd_lru[m, pl.ds(base - 3, LRU_R), :] * cw[0:1]
    xc = xc + rd_lru[m, pl.ds(base - 2, LRU_R), :] * cw[1:2]
    xc = xc + rd_lru[m, pl.ds(base - 1, LRU_R), :] * cw[2:3]
    xc = xc + rd_lru[m, pl.ds(base, LRU_R), :] * cw[3:4]
    gates = jnp.dot(xc.astype(BF16), wcat, preferred_element_type=F32)
    tr = jnp.tanh(gates[:, :LANES] + ba)
    ti = jnp.tanh(gates[:, LANES:] + bx)
    log_a = (tr + 1.0) * lsl4
    a = jnp.exp(log_a)
    v = -jnp.tanh(log_a)
    coef = jnp.where(v > 0.0, v * lax.rsqrt((v + v) * (1.0 + v)), 0.0)
    b = coef * ((ti + 1.0) * xc)
    a3 = a.reshape(nt, SUBLANES, LANES)
    b3 = b.reshape(nt, SUBLANES, LANES)
    for s in (1, 2, 4):
        sa_ref[slot, :, SUBLANES:2 * SUBLANES, :] = a3
        sb_ref[slot, :, SUBLANES:2 * SUBLANES, :] = b3
        a_sh = sa_ref[slot, :, SUBLANES - s:2 * SUBLANES - s, :]
        b_sh = sb_ref[slot, :, SUBLANES - s:2 * SUBLANES - s, :]
        b3 = a3 * b_sh + b3
        a3 = a3 * a_sh
    tiles = []
    for t in range(nt):
        ht = a3[t] * hc + b3[t]
        hc = ht[SUBLANES - 1:SUBLANES, :]
        tiles.append(ht)
    h = jnp.concatenate(tiles, axis=0)
    gelu_g = rd_rest[m, pl.ds(r0, LRU_R), :].astype(F32)
    y_ref[m, pl.ds(r0, LRU_R), :] = (h * gelu_g).astype(BF16)
    return hc


def _mix_body(x_hbm, ln_ref, wx_ref, wlg_ref, wq_ref, wk_ref, wv_ref, wrg_ref,
              unit_ref, wg_ref, rot_ref, base_ref, dec_ref, xz_ref,
              fg_ref, fu_ref, fd_ref, fo_ref,
              y_ref, bgu_ref, bd_ref, bo_ref,
              u_ref, plru_a, plru_b, prest_a, prest_b, sa_ref, sb_ref, h_ref, state_ref,
              cq_ref, sq_ref, ck_ref, sk_ref, xbuf_ref, xsem):
    i = pl.program_id(0)
    j = pl.program_id(1)
    nt = LRU_R // SUBLANES

    for f in range(D_FF // FFN_TF):
        bgu_ref[f, :, 0:FFN_TF] = fg_ref[:, f * FFN_TF:(f + 1) * FFN_TF].astype(BF16)
        bgu_ref[f, :, FFN_TF:2 * FFN_TF] = fu_ref[:, f * FFN_TF:(f + 1) * FFN_TF].astype(BF16)
    bd_ref[...] = fd_ref[...].astype(BF16)
    bo_ref[...] = fo_ref[...].astype(BF16)

    @pl.when(jnp.logical_and(i == 0, j == 0))
    def _():
        h_ref[...] = jnp.zeros_like(h_ref)
        state_ref[...] = jnp.zeros_like(state_ref)
        plru_a[:, 0:SUBLANES, :] = jnp.zeros((N_LRU_LB, SUBLANES, LANES), F32)

    def x_copy(blk):
        return pltpu.make_async_copy(x_hbm.at[pl.ds(blk * MIX_TS, MIX_TS), :], xbuf_ref, xsem)

    @pl.when(jnp.logical_and(i == 0, j == 0))
    def _():
        x_copy(0).start()

    @pl.when(jnp.logical_and(j == 0, i < N_BLK))
    def _():
        x_copy(i).wait()

        def step(r, carry):
            r0 = pl.multiple_of(r * NORM_ROWS, NORM_ROWS)
            x = xbuf_ref[pl.ds(r0, NORM_ROWS), :]
            u_ref[pl.ds(r0, NORM_ROWS), :] = _rmsnorm_rows(x, ln_ref[...]).astype(BF16)
            return carry

        lax.fori_loop(0, MIX_TS // NORM_ROWS, step, 0)

    @pl.when(jnp.logical_and(j == 1, i + 1 < N_BLK))
    def _():
        x_copy(i + 1).start()

    @pl.when(j == 0)
    def _():
        blk = jnp.maximum(i - 1, 0)
        cb_, sb_ = base_ref[blk][0:1], base_ref[blk][1:2]
        cos = cb_ * rot_ref[0] - sb_ * rot_ref[1]
        sin = sb_ * rot_ref[0] + cb_ * rot_ref[1]
        lane = lax.broadcasted_iota(jnp.int32, (1, RET_HD), 1)
        sin = jnp.where(lane < RET_HD // 2, -sin, sin)
        cq_ref[...] = cos
        sq_ref[...] = sin
        ck_ref[...] = cos * (RET_HD ** -0.5)
        sk_ref[...] = sin * (RET_HD ** -0.5)

    def head_groups(rd_lru, rd_rest, wr_lru):
        sa_ref[:, :, 0:SUBLANES, :] = jnp.ones((UNITS * LRU_SUB, nt, SUBLANES, LANES), F32)
        sb_ref[:, :, 0:SUBLANES, :] = jnp.zeros((UNITS * LRU_SUB, nt, SUBLANES, LANES), F32)
        for uu in range(UNITS):
            head_group(j * UNITS + uu, uu * LRU_SUB, rd_lru, rd_rest, wr_lru)

    def head_group(m, slot0, rd_lru, rd_rest, wr_lru):
        unit = unit_ref[m]
        cw, cb = unit[0:CONV_W], unit[U_CONV_B:U_CONV_B + 1]
        ba, bx = 0.5 * unit[U_GATE_A_B:U_GATE_A_B + 1], 0.5 * unit[U_GATE_X_B:U_GATE_X_B + 1]
        lam = unit[U_LAMBDA:U_LAMBDA + 1]
        gc, gnw = unit[U_CHUNK_DECAY:U_CHUNK_DECAY + 1], unit[U_GN_W:U_GN_W + 1]
        lsl4 = -4.0 * (jnp.maximum(-lam, 0.0) + jnp.log1p(jnp.exp(-jnp.abs(lam))))
        hc = h_ref[m][0:1, :]
        for slot in range(LRU_SUB):
            hc = _lru_sub_block(slot * LRU_R, hc, slot0 + slot, m, rd_lru, rd_rest, y_ref, sa_ref, sb_ref,
                                cw, cb, wg_ref[m], ba, bx, lsl4)
        h_ref[m] = jnp.broadcast_to(hc, (SUBLANES, LANES))
        wr_lru[m, 0:SUBLANES, :] = rd_lru[m, pl.ds(MIX_TS, SUBLANES), :]

        state = state_ref[m]
        for c in range(MIX_TS // RET_C):
            rows = slice(c * RET_C, (c + 1) * RET_C)
            q = rd_rest[8 + m, rows, :].astype(F32)
            k = rd_rest[16 + m, rows, :].astype(F32)
            vb = rd_rest[24 + m, rows, :]
            silu_g = rd_rest[32 + m, rows, :].astype(F32)
            qr = q * cq_ref[rows, :] + pltpu.roll(q, RET_HD // 2, 1) * sq_ref[rows, :]
            kr = k * ck_ref[rows, :] + pltpu.roll(k, RET_HD // 2, 1) * sk_ref[rows, :]
            scores = lax.dot_general(qr.astype(BF16), kr.astype(BF16), (((1,), (1,)), ((), ())),
                                     preferred_element_type=F32)
            p = (scores * dec_ref[m]).astype(BF16)
            lhs = jnp.concatenate([p, (qr * xz_ref[0, m]).astype(BF16)], axis=1)
            rhs = jnp.concatenate([vb, state.astype(BF16)], axis=0)
            o = jnp.dot(lhs, rhs, preferred_element_type=F32)
            kz = (kr * xz_ref[1, m]).astype(BF16)
            kv = lax.dot_general(kz, vb, (((0,), (0,)), ((), ())), preferred_element_type=F32)
            state = state * gc + kv
            mu = jnp.mean(o, axis=-1, keepdims=True)
            d = o - mu
            var = jnp.mean(d * d, axis=-1, keepdims=True)
            on = (d * lax.rsqrt(var + EPS)) * gnw
            y_ref[N_LRU_LB + m, rows, :] = (silu_g * on).astype(BF16)
        state_ref[m] = state

    def projection(wr_lru, wr_rest):
        w = jnp.concatenate([r[...] for r in (wx_ref, wlg_ref, wq_ref, wk_ref, wv_ref, wrg_ref)], axis=1)
        res = jnp.dot(u_ref[...], w, preferred_element_type=F32)
        for kk in range(MIX_LB):
            cls, unit = kk // UNITS, j * UNITS + kk % UNITS
            blk = res[:, kk * LANES:(kk + 1) * LANES]
            if cls == C_LRU_X:
                wr_lru[unit, pl.ds(SUBLANES, MIX_TS), :] = blk
                continue
            if cls == C_LRU_GATE:
                hg = 0.5 * blk
                blk = hg * jnp.tanh(blk * (GELU_C + (GELU_C * 0.044715) * (blk * blk))) + hg
            elif cls == C_RET_GATE:
                blk = _silu(blk)
            wr_rest[(cls - 1) * N_LRU_LB + unit, :, :] = blk.astype(BF16)

    first, last, even = i == 0, i == N_BLK, lax.rem(i, 2) == 0
    mid = jnp.logical_not(jnp.logical_or(first, last))

    @pl.when(first)
    def _():
        projection(plru_a, prest_a)

    @pl.when(jnp.logical_and(mid, even))
    def _():
        head_groups(plru_b, prest_b, plru_a)
        projection(plru_a, prest_a)

    @pl.when(jnp.logical_and(mid, jnp.logical_not(even)))
    def _():
        head_groups(plru_a, prest_a, plru_b)
        projection(plru_b, prest_b)

    @pl.when(last)
    def _():
        head_groups(plru_b, prest_b, plru_a)


def _mix(x, ln, w_slabs, unit_tab, wcat, rot, base, decay, xz, w_gate, w_up, w_down, w_out):
    assert N_BLK % 2 == 0
    prev_blk = lambda i, j: (jnp.maximum(i - 1, 0), 0, 0, 0)
    const = lambda a: pl.BlockSpec(a.shape, lambda i, j: (0,) * a.ndim, pipeline_mode=pl.Buffered(1))
    chunk = lambda i, j, n: jnp.minimum(i * MIX_J + j, n - 1)
    nt = LRU_R // SUBLANES
    n_steps = N_BLK * MIX_J
    up_rows = D_MODEL // n_steps
    down_rows = 4 * up_rows
    n_up, n_down, n_ff = D_MODEL // up_rows, D_FF // down_rows, D_FF // FFN_TF
    up_in = pl.BlockSpec((up_rows, D_FF), lambda i, j: (chunk(i, j, n_up), 0))
    up_out = pl.BlockSpec((n_ff, up_rows, 2 * FFN_TF), lambda i, j: (0, chunk(i, j, n_up), 0))
    down_io = pl.BlockSpec((down_rows, D_MODEL), lambda i, j: (chunk(i, j, n_down), 0))
    out_io = pl.BlockSpec((up_rows, D_MODEL), lambda i, j: (chunk(i, j, n_up), 0))
    lru_buf = pltpu.VMEM((N_LRU_LB, SUBLANES + MIX_TS, LANES), F32)
    rest_buf = pltpu.VMEM((N_REST_LB, MIX_TS, LANES), BF16)
    return pl.pallas_call(
        _mix_body,
        grid=(N_BLK + 1, MIX_J),
        in_specs=[
            pl.BlockSpec(memory_space=pl.ANY),
            pl.BlockSpec((1, D_MODEL), lambda i, j: (0, 0)),
            *[pl.BlockSpec((D_MODEL, UNITS * LANES), lambda i, j, c=c: (0, c * MIX_J + j))
              for c in range(D_IN // D_LRU)],
            const(unit_tab), const(wcat), const(rot), const(base), const(decay), const(xz),
            up_in, up_in, down_io, out_io,
        ],
        out_specs=[pl.BlockSpec((None, 2 * N_LRU_LB, MIX_TS, LANES), prev_blk), up_out, down_io, out_io],
        out_shape=[
            jax.ShapeDtypeStruct((N_BLK, 2 * N_LRU_LB, MIX_TS, LANES), BF16),
            jax.ShapeDtypeStruct((n_ff, D_MODEL, 2 * FFN_TF), BF16),
            jax.ShapeDtypeStruct(w_down.shape, BF16),
            jax.ShapeDtypeStruct(w_out.shape, BF16),
        ],
        scratch_shapes=[
            pltpu.VMEM((MIX_TS, D_MODEL), BF16),
            lru_buf, lru_buf, rest_buf, rest_buf,
            pltpu.VMEM((UNITS * LRU_SUB, nt, 2 * SUBLANES, LANES), F32),
            pltpu.VMEM((UNITS * LRU_SUB, nt, 2 * SUBLANES, LANES), F32),
            pltpu.VMEM((N_LRU_LB, SUBLANES, LANES), F32),
            pltpu.VMEM((RET_HEADS, RET_HD, RET_HD), F32),
            pltpu.VMEM((MIX_TS, RET_HD), F32),
            pltpu.VMEM((MIX_TS, RET_HD), F32),
            pltpu.VMEM((MIX_TS, RET_HD), F32),
            pltpu.VMEM((MIX_TS, RET_HD), F32),
            pltpu.VMEM((MIX_TS, D_MODEL), F32),
            pltpu.SemaphoreType.DMA(()),
        ],
        compiler_params=_params("arbitrary", "arbitrary"),
        name="mix",
    )(x, ln, *[w_slabs] * (D_IN // D_LRU), unit_tab, wcat, rot, base, decay, xz, w_gate, w_up, w_down, w_out)


def _outproj_body(x_ref, y_ref, wo_ref, o_ref):
    for rb in range(OUT_TS // MIX_TS):
        rows = slice(rb * MIX_TS, (rb + 1) * MIX_TS)
        y = jnp.concatenate([y_ref[rb, lb] for lb in range(2 * N_LRU_LB)], axis=1)
        o_ref[rows, :] = x_ref[rows, :] + jnp.dot(y, wo_ref[...], preferred_element_type=F32)


def _outproj(x, y, w_out):
    return pl.pallas_call(
        _outproj_body,
        grid=(SEQ // OUT_TS,),
        in_specs=[
            pl.BlockSpec((OUT_TS, D_MODEL), lambda i: (i, 0)),
            pl.BlockSpec((OUT_TS // MIX_TS, 2 * N_LRU_LB, MIX_TS, LANES), lambda i: (i, 0, 0, 0)),
            pl.BlockSpec((D_MODEL, D_MODEL), lambda i: (0, 0)),
        ],
        out_specs=pl.BlockSpec((OUT_TS, D_MODEL), lambda i: (i, 0)),
        out_shape=jax.ShapeDtypeStruct((SEQ, D_MODEL), F32),
        compiler_params=_params("parallel"),
        name="outproj",
    )(x, y, w_out)


def _ffn_body(h_hbm, ln2_ref, wgu_ref, wd_ref, fnw_ref, o_ref, u_ref, hbuf_ref, hsem):
    i = pl.program_id(0)
    f = pl.program_id(1)

    def h_copy(blk):
        return pltpu.make_async_copy(h_hbm.at[pl.ds(blk * FFN_TS, FFN_TS), :], hbuf_ref, hsem)

    @pl.when(jnp.logical_and(i == 0, f == 0))
    def _():
        h_copy(0).start()

    @pl.when(f == 0)
    def _():
        h_copy(i).wait()

        def step(c, carry):
            r0 = pl.multiple_of(c * NORM_ROWS, NORM_ROWS)
            hrows = hbuf_ref[pl.ds(r0, NORM_ROWS), :]
            u_ref[pl.ds(r0, NORM_ROWS), :] = _rmsnorm_rows(hrows, ln2_ref[...]).astype(BF16)
            o_ref[pl.ds(r0, NORM_ROWS), :] = hrows
            return carry

        lax.fori_loop(0, FFN_TS // NORM_ROWS, step, 0)

    @pl.when(jnp.logical_and(f == 1, i + 1 < pl.num_programs(0)))
    def _():
        h_copy(i + 1).start()

    for half in range(FFN_TS // FFN_HALF):
        rows = slice(half * FFN_HALF, (half + 1) * FFN_HALF)
        gu = jnp.dot(u_ref[rows, :], wgu_ref[...], preferred_element_type=F32)
        gate, up = gu[:, :FFN_TF], gu[:, FFN_TF:]
        act = (_silu(gate) * up).astype(BF16)
        o_ref[rows, :] += jnp.dot(act, wd_ref[...], preferred_element_type=F32)

    @pl.when(f == pl.num_programs(1) - 1)
    def _():
        def step(c, carry):
            r0 = pl.multiple_of(c * NORM_ROWS, NORM_ROWS)
            rows = pl.ds(r0, NORM_ROWS)
            o_ref[rows, :] = _rmsnorm_rows(o_ref[rows, :], fnw_ref[...])
            return carry

        lax.fori_loop(0, FFN_TS // NORM_ROWS, step, 0)


def _ffn(h1, ln2, wgu, wd, fnw):
    return pl.pallas_call(
        _ffn_body,
        grid=(SEQ // FFN_TS, D_FF // FFN_TF),
        in_specs=[
            pl.BlockSpec(memory_space=pl.ANY),
            pl.BlockSpec((1, D_MODEL), lambda i, f: (0, 0)),
            pl.BlockSpec((None, D_MODEL, 2 * FFN_TF), lambda i, f: (f, 0, 0)),
            pl.BlockSpec((FFN_TF, D_MODEL), lambda i, f: (f, 0)),
            pl.BlockSpec((1, D_MODEL), lambda i, f: (0, 0)),
        ],
        out_specs=pl.BlockSpec((FFN_TS, D_MODEL), lambda i, f: (i, 0)),
        out_shape=jax.ShapeDtypeStruct((SEQ, D_MODEL), F32),
        scratch_shapes=[
            pltpu.VMEM((FFN_TS, D_MODEL), BF16),
            pltpu.VMEM((FFN_TS, D_MODEL), F32),
            pltpu.SemaphoreType.DMA(()),
        ],
        compiler_params=_params("arbitrary", "arbitrary"),
        name="ffn",
    )(h1, ln2, wgu, wd, fnw)


def _retention_tables():
    H, Dh, C = RET_HEADS, RET_HD, RET_C
    inv_freq = ROPE_BASE ** (-np.arange(0, Dh, 2, dtype=np.float64) / Dh)
    inv2 = np.concatenate([inv_freq, inv_freq])
    off = np.arange(MIX_TS, dtype=np.float64)[:, None] * inv2[None, :]
    base = (np.arange(N_BLK, dtype=np.float64) * MIX_TS)[:, None, None] * inv2[None, None, :]
    log_gamma = np.log1p(-np.exp2(-5.0 - np.arange(H, dtype=np.float64)))
    idx = np.arange(C)
    diff = idx[:, None] - idx[None, :]
    decay = np.where(diff >= 0, np.exp(log_gamma[:, None, None] * np.maximum(diff, 0)[None]), 0.0)
    zeta = np.exp(log_gamma[:, None] * (C - 1 - idx)[None, :])
    xi = np.exp(log_gamma[:, None] * (idx + 1)[None, :])
    gc = np.exp(log_gamma * C)
    per_head = lambda t: np.broadcast_to(t[:, :, None], (H, C, Dh))
    tables = (np.stack([np.cos(off), np.sin(off)]),
              np.concatenate([np.cos(base), np.sin(base)], axis=1),
              decay,
              np.stack([per_head(xi), per_head(zeta)]),
              np.broadcast_to(gc[:, None, None], (H, 1, Dh)))
    return tuple(jnp.asarray(np.ascontiguousarray(t, dtype=np.float32)) for t in tables)


def kernel(x, ln1_w, w_in, conv_w, conv_b, gate_a_w, gate_a_b, gate_x_w, gate_x_b, lru_lambda, ret_gn_w,
           w_out, ln2_w, w_ffn_gate, w_ffn_up, w_ffn_down, final_norm_w):
    x2 = x.reshape(SEQ, D_MODEL)
    row = lambda v: v.reshape(1, -1)
    by_lb = lambda v: v.reshape(-1, 1, LANES)
    wcat = (0.5 * jnp.concatenate([gate_a_w[0], gate_x_w[0]], axis=-1)).astype(BF16)
    cw = conv_w[0].reshape(CONV_W, N_LRU_LB, LANES).transpose(1, 0, 2)
    rot, base, decay, xz, gc = _retention_tables()
    unit_tab = jnp.concatenate(
        [cw, by_lb(conv_b[0]), by_lb(gate_a_b[0]), by_lb(gate_x_b[0]), by_lb(lru_lambda[0]), gc,
         by_lb(ret_gn_w[0]), jnp.zeros((N_LRU_LB, UNIT_ROWS - U_GN_W - 1, LANES), F32)], axis=1)
    y, wgu, wd, wo = _mix(x2, row(ln1_w[0]), w_in[0].astype(BF16), unit_tab, wcat, rot, base, decay, xz,
                          w_ffn_gate[0], w_ffn_up[0], w_ffn_down[0], w_out[0])
    h1 = _outproj(x2, y, wo)
    out = _ffn(h1, row(ln2_w[0]), wgu, wd, row(final_norm_w))
    return out.reshape(1, SEQ, D_MODEL)
```

```python
import jax
import jax.numpy as jnp
import numpy as np
from jax import lax
from jax.experimental import pallas as pl
from jax.experimental.pallas import tpu as pltpu

D_MODEL = 2048
SEQ = 8192
D_LRU = 1024
D_RET = 1024
CONV_W = 4
RET_HEADS = 8
RET_HD = 128
ROPE_BASE = 10000.0
D_FF = 5632
D_IN = 6144
EPS = 1e-6
GELU_C = 0.7978845608028654

SUBLANES = 8
LANES = 128
VMEM_LIMIT_BYTES = 56 * 1024 * 1024

F32 = jnp.float32
BF16 = jnp.bfloat16

NORM_ROWS = 128
MIX_TS = 512
MIX_J = 4
MIX_NW = D_IN // MIX_J
MIX_LB = MIX_NW // LANES
N_BLK = SEQ // MIX_TS
N_LRU_LB = D_LRU // LANES
UNITS = N_LRU_LB // MIX_J
C_LRU_X, C_LRU_GATE, C_Q, C_K, C_V, C_RET_GATE = range(6)
P_GELU, P_QR, P_QXI, P_KR, P_KZ, P_V, P_SILU = range(7)
N_BF16_KINDS = 7
LRU_R = 128
LRU_SUB = MIX_TS // LRU_R
RET_C = 256
OUT_TS = 1024
FFN_TS, FFN_TF = 1024, 512
FFN_HALF = 512
U_CONV_B, U_GATE_A_B, U_GATE_X_B, U_LAMBDA, U_CHUNK_DECAY, U_GN_W = 4, 5, 6, 7, 8, 9
UNIT_ROWS = 16


def _rmsnorm_rows(x, w):
    ms = jnp.mean(x * x, axis=-1, keepdims=True)
    return (x * lax.rsqrt(ms + EPS)) * w


def _silu(x):
    h = 0.5 * x
    return h * jnp.tanh(h) + h


def _params(*sem):
    return pltpu.CompilerParams(dimension_semantics=sem, vmem_limit_bytes=VMEM_LIMIT_BYTES)


def _lru_sub_block(r0, hc, slot, m, rd_lru, rd_rest, y_ref, sa_ref, sb_ref, cw, cb, wcat, ba, bx, lsl4):
    nt = LRU_R // SUBLANES
    base = r0 + SUBLANES
    xc = cb + rd_lru[m, pl.ds(base - 3, LRU_R), :] * cw[0:1]
    xc = xc + rd_lru[m, pl.ds(base - 2, LRU_R), :] * cw[1:2]
    xc = xc + rd_lru[m, pl.ds(base - 1, LRU_R), :] * cw[2:3]
    xc = xc + rd_lru[m, pl.ds(base, LRU_R), :] * cw[3:4]
    gates = jnp.dot(xc.astype(BF16), wcat, preferred_element_type=F32)
    tr = jnp.tanh(gates[:, :LANES] + ba)
    ti = jnp.tanh(gates[:, LANES:] + bx)
    log_a = (tr + 1.0) * lsl4
    a = jnp.exp(log_a)
    v = -jnp.tanh(log_a)
    coef = jnp.where(v > 0.0, v * lax.rsqrt((v + v) * (1.0 + v)), 0.0)
    b = coef * ((ti + 1.0) * xc)
    a3 = a.reshape(nt, SUBLANES, LANES)
    b3 = b.reshape(nt, SUBLANES, LANES)
    for s in (1, 2, 4):
        sa_ref[slot, :, SUBLANES:2 * SUBLANES, :] = a3
        sb_ref[slot, :, SUBLANES:2 * SUBLANES, :] = b3
        a_sh = sa_ref[slot, :, SUBLANES - s:2 * SUBLANES - s, :]
        b_sh = sb_ref[slot, :, SUBLANES - s:2 * SUBLANES - s, :]
        b3 = a3 * b_sh + b3
        a3 = a3 * a_sh
    tiles = []
    for t in range(nt):
        ht = a3[t] * hc + b3[t]
        hc = ht[SUBLANES - 1:SUBLANES, :]
        tiles.append(ht)
    h = jnp.concatenate(tiles, axis=0)
    gelu_g = rd_rest[m, pl.ds(r0, LRU_R), :].astype(F32)
    y_ref[m, pl.ds(r0, LRU_R), :] = (h * gelu_g).astype(BF16)
    return hc


def _mix_body(x_hbm, ln_ref, wx_ref, wlg_ref, wq_ref, wk_ref, wv_ref, wrg_ref,
              unit_ref, wg_ref, rot_ref, base_ref, dec_ref, xz_ref,
              fg_ref, fu_ref, fd_ref, fo_ref,
              y_ref, bgu_ref, bd_ref, bo_ref,
              u_ref, plru_ref, prest_ref, sa_ref, sb_ref, h_ref, state_ref,
              cq_ref, sq_ref, ck_ref, sk_ref, xbuf_ref, xsem):
    i = pl.program_id(0)
    j = pl.program_id(1)
    nt = LRU_R // SUBLANES

    for f in range(D_FF // FFN_TF):
        bgu_ref[f, :, 0:FFN_TF] = fg_ref[:, f * FFN_TF:(f + 1) * FFN_TF].astype(BF16)
        bgu_ref[f, :, FFN_TF:2 * FFN_TF] = fu_ref[:, f * FFN_TF:(f + 1) * FFN_TF].astype(BF16)
    bd_ref[...] = fd_ref[...].astype(BF16)
    bo_ref[...] = fo_ref[...].astype(BF16)

    @pl.when(jnp.logical_and(i == 0, j == 0))
    def _():
        h_ref[...] = jnp.zeros_like(h_ref)
        state_ref[...] = jnp.zeros_like(state_ref)
        plru_ref[:, 0:SUBLANES, :] = jnp.zeros((N_LRU_LB, SUBLANES, LANES), F32)

    def x_copy(blk):
        return pltpu.make_async_copy(x_hbm.at[pl.ds(blk * MIX_TS, MIX_TS), :], xbuf_ref, xsem)

    @pl.when(jnp.logical_and(i == 0, j == 0))
    def _():
        x_copy(0).start()

    @pl.when(jnp.logical_and(j == 0, i < N_BLK))
    def _():
        x_copy(i).wait()

        def step(r, carry):
            r0 = pl.multiple_of(r * NORM_ROWS, NORM_ROWS)
            x = xbuf_ref[pl.ds(r0, NORM_ROWS), :]
            u_ref[pl.ds(r0, NORM_ROWS), :] = _rmsnorm_rows(x, ln_ref[...]).astype(BF16)
            return carry

        lax.fori_loop(0, MIX_TS // NORM_ROWS, step, 0)

    @pl.when(jnp.logical_and(j == 1, i + 1 < N_BLK))
    def _():
        x_copy(i + 1).start()

    @pl.when(jnp.logical_and(j == 0, i < N_BLK))
    def _():
        cb_, sb_ = base_ref[i][0:1], base_ref[i][1:2]
        cos = cb_ * rot_ref[0] - sb_ * rot_ref[1]
        sin = sb_ * rot_ref[0] + cb_ * rot_ref[1]
        lane = lax.broadcasted_iota(jnp.int32, (1, RET_HD), 1)
        sin = jnp.where(lane < RET_HD // 2, -sin, sin)
        cq_ref[...] = cos
        sq_ref[...] = sin
        ck_ref[...] = cos * (RET_HD ** -0.5)
        sk_ref[...] = sin * (RET_HD ** -0.5)

    def head_groups():
        sa_ref[:, :, 0:SUBLANES, :] = jnp.ones((UNITS * LRU_SUB, nt, SUBLANES, LANES), F32)
        sb_ref[:, :, 0:SUBLANES, :] = jnp.zeros((UNITS * LRU_SUB, nt, SUBLANES, LANES), F32)
        for uu in range(UNITS):
            head_group(j * UNITS + uu, uu * LRU_SUB)

    def head_group(m, slot0):
        unit = unit_ref[m]
        cw, cb = unit[0:CONV_W], unit[U_CONV_B:U_CONV_B + 1]
        ba, bx = 0.5 * unit[U_GATE_A_B:U_GATE_A_B + 1], 0.5 * unit[U_GATE_X_B:U_GATE_X_B + 1]
        lam = unit[U_LAMBDA:U_LAMBDA + 1]
        gc, gnw = unit[U_CHUNK_DECAY:U_CHUNK_DECAY + 1], unit[U_GN_W:U_GN_W + 1]
        lsl4 = -4.0 * (jnp.maximum(-lam, 0.0) + jnp.log1p(jnp.exp(-jnp.abs(lam))))
        hc = h_ref[m][0:1, :]
        for slot in range(LRU_SUB):
            hc = _lru_sub_block(slot * LRU_R, hc, slot0 + slot, m, plru_ref, prest_ref, y_ref, sa_ref, sb_ref,
                                cw, cb, wg_ref[m], ba, bx, lsl4)
        h_ref[m] = jnp.broadcast_to(hc, (SUBLANES, LANES))
        plru_ref[m, 0:SUBLANES, :] = plru_ref[m, pl.ds(MIX_TS, SUBLANES), :]

        state = state_ref[m]
        for c in range(MIX_TS // RET_C):
            rows = slice(c * RET_C, (c + 1) * RET_C)
            qb = prest_ref[P_QR * N_LRU_LB + m, rows, :]
            qxi = prest_ref[P_QXI * N_LRU_LB + m, rows, :]
            kb = prest_ref[P_KR * N_LRU_LB + m, rows, :]
            kz = prest_ref[P_KZ * N_LRU_LB + m, rows, :]
            vb = prest_ref[P_V * N_LRU_LB + m, rows, :]
            silu_g = prest_ref[P_SILU * N_LRU_LB + m, rows, :].astype(F32)
            scores = lax.dot_general(qb, kb, (((1,), (1,)), ((), ())), preferred_element_type=F32)
            p = (scores * dec_ref[m]).astype(BF16)
            lhs = jnp.concatenate([p, qxi], axis=1)
            rhs = jnp.concatenate([vb, state.astype(BF16)], axis=0)
            o = jnp.dot(lhs, rhs, preferred_element_type=F32)
            kv = lax.dot_general(kz, vb, (((0,), (0,)), ((), ())), preferred_element_type=F32)
            state = state * gc + kv
            mu = jnp.mean(o, axis=-1, keepdims=True)
            d = o - mu
            var = jnp.mean(d * d, axis=-1, keepdims=True)
            on = (d * lax.rsqrt(var + EPS)) * gnw
            y_ref[N_LRU_LB + m, rows, :] = (silu_g * on).astype(BF16)
        state_ref[m] = state

    def projection():
        w = jnp.concatenate([r[...] for r in (wx_ref, wlg_ref, wq_ref, wk_ref, wv_ref, wrg_ref)], axis=1)
        res = jnp.dot(u_ref[...], w, preferred_element_type=F32)
        for kk in range(MIX_LB):
            cls, unit = kk // UNITS, j * UNITS + kk % UNITS
            blk = res[:, kk * LANES:(kk + 1) * LANES]
            if cls == C_LRU_X:
                plru_ref[unit, pl.ds(SUBLANES, MIX_TS), :] = blk
            elif cls == C_LRU_GATE:
                hg = 0.5 * blk
                gelu = hg * jnp.tanh(blk * (GELU_C + (GELU_C * 0.044715) * (blk * blk))) + hg
                prest_ref[P_GELU * N_LRU_LB + unit, :, :] = gelu.astype(BF16)
            elif cls in (C_Q, C_K):
                cos, sin = (cq_ref, sq_ref) if cls == C_Q else (ck_ref, sk_ref)
                rot = blk * cos[...] + pltpu.roll(blk, RET_HD // 2, 1) * sin[...]
                factor = jnp.concatenate([xz_ref[0 if cls == C_Q else 1, unit]] * (MIX_TS // RET_C), axis=0)
                p_rot, p_scaled = (P_QR, P_QXI) if cls == C_Q else (P_KR, P_KZ)
                prest_ref[p_rot * N_LRU_LB + unit, :, :] = rot.astype(BF16)
                prest_ref[p_scaled * N_LRU_LB + unit, :, :] = (rot * factor).astype(BF16)
            elif cls == C_V:
                prest_ref[P_V * N_LRU_LB + unit, :, :] = blk.astype(BF16)
            else:
                prest_ref[P_SILU * N_LRU_LB + unit, :, :] = _silu(blk).astype(BF16)

    first, last = i == 0, i == N_BLK

    @pl.when(first)
    def _():
        projection()

    @pl.when(jnp.logical_not(jnp.logical_or(first, last)))
    def _():
        head_groups()
        projection()

    @pl.when(last)
    def _():
        head_groups()


def _mix(x, ln, w_slabs, unit_tab, wcat, rot, base, decay, xz, w_gate, w_up, w_down, w_out):
    prev_blk = lambda i, j: (jnp.maximum(i - 1, 0), 0, 0, 0)
    const = lambda a: pl.BlockSpec(a.shape, lambda i, j: (0,) * a.ndim, pipeline_mode=pl.Buffered(1))
    chunk = lambda i, j, n: jnp.minimum(i * MIX_J + j, n - 1)
    nt = LRU_R // SUBLANES
    n_steps = N_BLK * MIX_J
    up_rows = D_MODEL // n_steps
    down_rows = 4 * up_rows
    n_up, n_down, n_ff = D_MODEL // up_rows, D_FF // down_rows, D_FF // FFN_TF
    up_in = pl.BlockSpec((up_rows, D_FF), lambda i, j: (chunk(i, j, n_up), 0))
    up_out = pl.BlockSpec((n_ff, up_rows, 2 * FFN_TF), lambda i, j: (0, chunk(i, j, n_up), 0))
    down_io = pl.BlockSpec((down_rows, D_MODEL), lambda i, j: (chunk(i, j, n_down), 0))
    out_io = pl.BlockSpec((up_rows, D_MODEL), lambda i, j: (chunk(i, j, n_up), 0))
    lru_buf = pltpu.VMEM((N_LRU_LB, SUBLANES + MIX_TS, LANES), F32)
    rest_buf = pltpu.VMEM((N_BF16_KINDS * N_LRU_LB, MIX_TS, LANES), BF16)
    return pl.pallas_call(
        _mix_body,
        grid=(N_BLK + 1, MIX_J),
        in_specs=[
            pl.BlockSpec(memory_space=pl.ANY),
            pl.BlockSpec((1, D_MODEL), lambda i, j: (0, 0)),
            *[pl.BlockSpec((D_MODEL, UNITS * LANES), lambda i, j, c=c: (0, c * MIX_J + j))
              for c in range(D_IN // D_LRU)],
            const(unit_tab), const(wcat), const(rot), const(base), const(decay), const(xz),
            up_in, up_in, down_io, out_io,
        ],
        out_specs=[pl.BlockSpec((None, 2 * N_LRU_LB, MIX_TS, LANES), prev_blk), up_out, down_io, out_io],
        out_shape=[
            jax.ShapeDtypeStruct((N_BLK, 2 * N_LRU_LB, MIX_TS, LANES), BF16),
            jax.ShapeDtypeStruct((n_ff, D_MODEL, 2 * FFN_TF), BF16),
            jax.ShapeDtypeStruct(w_down.shape, BF16),
            jax.ShapeDtypeStruct(w_out.shape, BF16),
        ],
        scratch_shapes=[
            pltpu.VMEM((MIX_TS, D_MODEL), BF16),
            lru_buf, rest_buf,
            pltpu.VMEM((UNITS * LRU_SUB, nt, 2 * SUBLANES, LANES), F32),
            pltpu.VMEM((UNITS * LRU_SUB, nt, 2 * SUBLANES, LANES), F32),
            pltpu.VMEM((N_LRU_LB, SUBLANES, LANES), F32),
            pltpu.VMEM((RET_HEADS, RET_HD, RET_HD), F32),
            pltpu.VMEM((MIX_TS, RET_HD), F32),
            pltpu.VMEM((MIX_TS, RET_HD), F32),
            pltpu.VMEM((MIX_TS, RET_HD), F32),
            pltpu.VMEM((MIX_TS, RET_HD), F32),
            pltpu.VMEM((MIX_TS, D_MODEL), F32),
            pltpu.SemaphoreType.DMA(()),
        ],
        compiler_params=_params("arbitrary", "arbitrary"),
        name="mix",
    )(x, ln, *[w_slabs] * (D_IN // D_LRU), unit_tab, wcat, rot, base, decay, xz, w_gate, w_up, w_down, w_out)


def _outproj_body(x_ref, y_ref, wo_ref, o_ref):
    for rb in range(OUT_TS // MIX_TS):
        rows = slice(rb * MIX_TS, (rb + 1) * MIX_TS)
        y = jnp.concatenate([y_ref[rb, lb] for lb in range(2 * N_LRU_LB)], axis=1)
        o_ref[rows, :] = x_ref[rows, :] + jnp.dot(y, wo_ref[...], preferred_element_type=F32)


def _outproj(x, y, w_out):
    return pl.pallas_call(
        _outproj_body,
        grid=(SEQ // OUT_TS,),
        in_specs=[
            pl.BlockSpec((OUT_TS, D_MODEL), lambda i: (i, 0)),
            pl.BlockSpec((OUT_TS // MIX_TS, 2 * N_LRU_LB, MIX_TS, LANES), lambda i: (i, 0, 0, 0)),
            pl.BlockSpec((D_MODEL, D_MODEL), lambda i: (0, 0)),
        ],
        out_specs=pl.BlockSpec((OUT_TS, D_MODEL), lambda i: (i, 0)),
        out_shape=jax.ShapeDtypeStruct((SEQ, D_MODEL), F32),
        compiler_params=_params("parallel"),
        name="outproj",
    )(x, y, w_out)


def _ffn_body(h_hbm, ln2_ref, wgu_ref, wd_ref, fnw_ref, o_ref, u_ref, hbuf_ref, hsem):
    i = pl.program_id(0)
    f = pl.program_id(1)

    def h_copy(blk):
        return pltpu.make_async_copy(h_hbm.at[pl.ds(blk * FFN_TS, FFN_TS), :], hbuf_ref, hsem)

    @pl.when(jnp.logical_and(i == 0, f == 0))
    def _():
        h_copy(0).start()

    @pl.when(f == 0)
    def _():
        h_copy(i).wait()

        def step(c, carry):
            r0 = pl.multiple_of(c * NORM_ROWS, NORM_ROWS)
            hrows = hbuf_ref[pl.ds(r0, NORM_ROWS), :]
            u_ref[pl.ds(r0, NORM_ROWS), :] = _rmsnorm_rows(hrows, ln2_ref[...]).astype(BF16)
            o_ref[pl.ds(r0, NORM_ROWS), :] = hrows
            return carry

        lax.fori_loop(0, FFN_TS // NORM_ROWS, step, 0)

    @pl.when(jnp.logical_and(f == 1, i + 1 < pl.num_programs(0)))
    def _():
        h_copy(i + 1).start()

    for half in range(FFN_TS // FFN_HALF):
        rows = slice(half * FFN_HALF, (half + 1) * FFN_HALF)
        gu = jnp.dot(u_ref[rows, :], wgu_ref[...], preferred_element_type=F32)
        gate, up = gu[:, :FFN_TF], gu[:, FFN_TF:]
        act = (_silu(gate) * up).astype(BF16)
        o_ref[rows, :] += jnp.dot(act, wd_ref[...], preferred_element_type=F32)

    @pl.when(f == pl.num_programs(1) - 1)
    def _():
        def step(c, carry):
            r0 = pl.multiple_of(c * NORM_ROWS, NORM_ROWS)
            rows = pl.ds(r0, NORM_ROWS)
            o_ref[rows, :] = _rmsnorm_rows(o_ref[rows, :], fnw_ref[...])
            return carry

        lax.fori_loop(0, FFN_TS // NORM_ROWS, step, 0)


def _ffn(h1, ln2, wgu, wd, fnw):
    return pl.pallas_call(
        _ffn_body,
        grid=(SEQ // FFN_TS, D_FF // FFN_TF),
        in_specs=[
            pl.BlockSpec(memory_space=pl.ANY),
            pl.BlockSpec((1, D_MODEL), lambda i, f: (0, 0)),
            pl.BlockSpec((None, D_MODEL, 2 * FFN_TF), lambda i, f: (f, 0, 0)),
            pl.BlockSpec((FFN_TF, D_MODEL), lambda i, f: (f, 0)),
            pl.BlockSpec((1, D_MODEL), lambda i, f: (0, 0)),
        ],
        out_specs=pl.BlockSpec((FFN_TS, D_MODEL), lambda i, f: (i, 0)),
        out_shape=jax.ShapeDtypeStruct((SEQ, D_MODEL), F32),
        scratch_shapes=[
            pltpu.VMEM((FFN_TS, D_MODEL), BF16),
            pltpu.VMEM((FFN_TS, D_MODEL), F32),
            pltpu.SemaphoreType.DMA(()),
        ],
        compiler_params=_params("arbitrary", "arbitrary"),
        name="ffn",
    )(h1, ln2, wgu, wd, fnw)


def _retention_tables():
    H, Dh, C = RET_HEADS, RET_HD, RET_C
    inv_freq = ROPE_BASE ** (-np.arange(0, Dh, 2, dtype=np.float64) / Dh)
    inv2 = np.concatenate([inv_freq, inv_freq])
    off = np.arange(MIX_TS, dtype=np.float64)[:, None] * inv2[None, :]
    base = (np.arange(N_BLK, dtype=np.float64) * MIX_TS)[:, None, None] * inv2[None, None, :]
    log_gamma = np.log1p(-np.exp2(-5.0 - np.arange(H, dtype=np.float64)))
    idx = np.arange(C)
    diff = idx[:, None] - idx[None, :]
    decay = np.where(diff >= 0, np.exp(log_gamma[:, None, None] * np.maximum(diff, 0)[None]), 0.0)
    zeta = np.exp(log_gamma[:, None] * (C - 1 - idx)[None, :])
    xi = np.exp(log_gamma[:, None] * (idx + 1)[None, :])
    gc = np.exp(log_gamma * C)
    per_head = lambda t: np.broadcast_to(t[:, :, None], (H, C, Dh))
    tables = (np.stack([np.cos(off), np.sin(off)]),
              np.concatenate([np.cos(base), np.sin(base)], axis=1),
              decay,
              np.stack([per_head(xi), per_head(zeta)]),
              np.broadcast_to(gc[:, None, None], (H, 1, Dh)))
    return tuple(jnp.asarray(np.ascontiguousarray(t, dtype=np.float32)) for t in tables)


def kernel(x, ln1_w, w_in, conv_w, conv_b, gate_a_w, gate_a_b, gate_x_w, gate_x_b, lru_lambda, ret_gn_w,
           w_out, ln2_w, w_ffn_gate, w_ffn_up, w_ffn_down, final_norm_w):
    x2 = x.reshape(SEQ, D_MODEL)
    row = lambda v: v.reshape(1, -1)
    by_lb = lambda v: v.reshape(-1, 1, LANES)
    wcat = (0.5 * jnp.concatenate([gate_a_w[0], gate_x_w[0]], axis=-1)).astype(BF16)
    cw = conv_w[0].reshape(CONV_W, N_LRU_LB, LANES).transpose(1, 0, 2)
    rot, base, decay, xz, gc = _retention_tables()
    unit_tab = jnp.concatenate(
        [cw, by_lb(conv_b[0]), by_lb(gate_a_b[0]), by_lb(gate_x_b[0]), by_lb(lru_lambda[0]), gc,
         by_lb(ret_gn_w[0]), jnp.zeros((N_LRU_LB, UNIT_ROWS - U_GN_W - 1, LANES), F32)], axis=1)
    y, wgu, wd, wo = _mix(x2, row(ln1_w[0]), w_in[0].astype(BF16), unit_tab, wcat, rot, base, decay, xz,
                          w_ffn_gate[0], w_ffn_up[0], w_ffn_down[0], w_out[0])
    h1 = _outproj(x2, y, wo)
    out = _ffn(h1, row(ln2_w[0]), wgu, wd, row(final_norm_w))
    return out.reshape(1, SEQ, D_MODEL)
```

```python
import jax
import jax.numpy as jnp
import numpy as np
from jax import lax
from jax.experimental import pallas as pl
from jax.experimental.pallas import tpu as pltpu

D_MODEL = 2048
SEQ = 8192
D_LRU = 1024
D_RET = 1024
CONV_W = 4
RET_HEADS = 8
RET_HD = 128
ROPE_BASE = 10000.0
D_FF = 5632
D_IN = 6144
EPS = 1e-6
GELU_C = 0.7978845608028654

SUBLANES = 8
LANES = 128
VMEM_LIMIT_BYTES = 56 * 1024 * 1024

F32 = jnp.float32
BF16 = jnp.bfloat16

NORM_ROWS = 128
MIX_TS = 512
MIX_J = 4
MIX_NW = D_IN // MIX_J
MIX_LB = MIX_NW // LANES
N_BLK = SEQ // MIX_TS
N_LRU_LB = D_LRU // LANES
UNITS = N_LRU_LB // MIX_J
C_LRU_X, C_LRU_GATE, C_Q, C_K, C_V, C_RET_GATE = range(6)
DOT_CLASS_ORDER = (C_Q, C_K, C_LRU_GATE, C_RET_GATE, C_LRU_X, C_V)
P_GELU, P_QR, P_QXI, P_KR, P_KZ, P_V, P_SILU = range(7)
N_BF16_KINDS = 7
LRU_R = 128
LRU_SUB = MIX_TS // LRU_R
RET_C = 256
_LRU_PAIR, _RET_PAIR = tuple(("lru", u) for u in range(UNITS)), tuple(("ret", u) for u in range(UNITS))
HEAD_GROUP_ORDER = _LRU_PAIR + _RET_PAIR + 2 * _LRU_PAIR + _RET_PAIR + _LRU_PAIR
OUT_TS = 1024
FFN_TS, FFN_TF = 1024, 512
FFN_HALF = 512
U_CONV_B, U_GATE_A_B, U_GATE_X_B, U_LAMBDA, U_CHUNK_DECAY, U_GN_W = 4, 5, 6, 7, 8, 9
UNIT_ROWS = 16


def _rmsnorm_rows(x, w):
    ms = jnp.mean(x * x, axis=-1, keepdims=True)
    return (x * lax.rsqrt(ms + EPS)) * w


def _silu(x):
    h = 0.5 * x
    return h * jnp.tanh(h) + h


def _params(*sem):
    return pltpu.CompilerParams(dimension_semantics=sem, vmem_limit_bytes=VMEM_LIMIT_BYTES)


def _lru_sub_block(r0, hc, slot, m, rd_lru, rd_rest, y_ref, sa_ref, sb_ref, cw, cb, wcat, ba, bx, lsl4):
    nt = LRU_R // SUBLANES
    base = r0 + SUBLANES
    xc = cb + rd_lru[m, pl.ds(base - 3, LRU_R), :] * cw[0:1]
    xc = xc + rd_lru[m, pl.ds(base - 2, LRU_R), :] * cw[1:2]
    xc = xc + rd_lru[m, pl.ds(base - 1, LRU_R), :] * cw[2:3]
    xc = xc + rd_lru[m, pl.ds(base, LRU_R), :] * cw[3:4]
    gates = jnp.dot(xc.astype(BF16), wcat, preferred_element_type=F32)
    tr = jnp.tanh(gates[:, :LANES] + ba)
    ti = jnp.tanh(gates[:, LANES:] + bx)
    log_a = (tr + 1.0) * lsl4
    a = jnp.exp(log_a)
    v = -jnp.tanh(log_a)
    coef = jnp.where(v > 0.0, v * lax.rsqrt((v + v) * (1.0 + v)), 0.0)
    b = coef * ((ti + 1.0) * xc)
    a3 = a.reshape(nt, SUBLANES, LANES)
    b3 = b.reshape(nt, SUBLANES, LANES)
    for s in (1, 2, 4):
        sa_ref[slot, :, SUBLANES:2 * SUBLANES, :] = a3
        sb_ref[slot, :, SUBLANES:2 * SUBLANES, :] = b3
        a_sh = sa_ref[slot, :, SUBLANES - s:2 * SUBLANES - s, :]
        b_sh = sb_ref[slot, :, SUBLANES - s:2 * SUBLANES - s, :]
        b3 = a3 * b_sh + b3
        a3 = a3 * a_sh
    tiles = []
    for t in range(nt):
        ht = a3[t] * hc + b3[t]
        hc = ht[SUBLANES - 1:SUBLANES, :]
        tiles.append(ht)
    h = jnp.concatenate(tiles, axis=0)
    gelu_g = rd_rest[m, pl.ds(r0, LRU_R), :].astype(F32)
    y_ref[m, pl.ds(r0, LRU_R), :] = (h * gelu_g).astype(BF16)
    return hc


def _mix_body(x_hbm, ln_ref, wx_ref, wlg_ref, wq_ref, wk_ref, wv_ref, wrg_ref,
              unit_ref, wg_ref, rot_ref, base_ref, dec_ref, xz_ref,
              fg_ref, fu_ref, fd_ref, fo_ref,
              y_ref, bgu_ref, bd_ref, bo_ref,
              u_ref, plru_ref, prest_ref, sa_ref, sb_ref, h_ref, state_ref,
              cq_ref, sq_ref, ck_ref, sk_ref, xbuf_ref, xsem):
    i = pl.program_id(0)
    j = pl.program_id(1)
    nt = LRU_R // SUBLANES

    for f in range(D_FF // FFN_TF):
        bgu_ref[f, :, 0:FFN_TF] = fg_ref[:, f * FFN_TF:(f + 1) * FFN_TF].astype(BF16)
        bgu_ref[f, :, FFN_TF:2 * FFN_TF] = fu_ref[:, f * FFN_TF:(f + 1) * FFN_TF].astype(BF16)
    bd_ref[...] = fd_ref[...].astype(BF16)
    bo_ref[...] = fo_ref[...].astype(BF16)

    @pl.when(jnp.logical_and(i == 0, j == 0))
    def _():
        h_ref[...] = jnp.zeros_like(h_ref)
        state_ref[...] = jnp.zeros_like(state_ref)
        plru_ref[:, 0:SUBLANES, :] = jnp.zeros((N_LRU_LB, SUBLANES, LANES), F32)

    def x_copy(blk):
        return pltpu.make_async_copy(x_hbm.at[pl.ds(blk * MIX_TS, MIX_TS), :], xbuf_ref, xsem)

    @pl.when(jnp.logical_and(i == 0, j == 0))
    def _():
        x_copy(0).start()

    @pl.when(jnp.logical_and(j == 0, i < N_BLK))
    def _():
        x_copy(i).wait()

        def step(r, carry):
            r0 = pl.multiple_of(r * NORM_ROWS, NORM_ROWS)
            x = xbuf_ref[pl.ds(r0, NORM_ROWS), :]
            u_ref[pl.ds(r0, NORM_ROWS), :] = _rmsnorm_rows(x, ln_ref[...]).astype(BF16)
            return carry

        lax.fori_loop(0, MIX_TS // NORM_ROWS, step, 0)

    @pl.when(jnp.logical_and(j == 1, i + 1 < N_BLK))
    def _():
        x_copy(i + 1).start()

    @pl.when(jnp.logical_and(j == 0, i < N_BLK))
    def _():
        cb_, sb_ = base_ref[i][0:1], base_ref[i][1:2]
        cos = cb_ * rot_ref[0] - sb_ * rot_ref[1]
        sin = sb_ * rot_ref[0] + cb_ * rot_ref[1]
        lane = lax.broadcasted_iota(jnp.int32, (1, RET_HD), 1)
        sin = jnp.where(lane < RET_HD // 2, -sin, sin)
        cq_ref[...] = cos
        sq_ref[...] = sin
        ck_ref[...] = cos * (RET_HD ** -0.5)
        sk_ref[...] = sin * (RET_HD ** -0.5)

    def head_groups():
        sa_ref[:, :, 0:SUBLANES, :] = jnp.ones((UNITS * LRU_SUB, nt, SUBLANES, LANES), F32)
        sb_ref[:, :, 0:SUBLANES, :] = jnp.zeros((UNITS * LRU_SUB, nt, SUBLANES, LANES), F32)
        parts = {}
        for uu in range(UNITS):
            parts["lru", uu] = lru_lane_block(j * UNITS + uu, uu * LRU_SUB)
            parts["ret", uu] = retention_head(j * UNITS + uu)
        for key in HEAD_GROUP_ORDER:
            next(parts[key], None)
        for rest in parts.values():
            for _ in rest:
                pass

    def lru_lane_block(m, slot0):
        unit = unit_ref[m]
        cw, cb = unit[0:CONV_W], unit[U_CONV_B:U_CONV_B + 1]
        ba, bx = 0.5 * unit[U_GATE_A_B:U_GATE_A_B + 1], 0.5 * unit[U_GATE_X_B:U_GATE_X_B + 1]
        lam = unit[U_LAMBDA:U_LAMBDA + 1]
        lsl4 = -4.0 * (jnp.maximum(-lam, 0.0) + jnp.log1p(jnp.exp(-jnp.abs(lam))))
        hc = h_ref[m][0:1, :]
        for slot in range(LRU_SUB):
            hc = _lru_sub_block(slot * LRU_R, hc, slot0 + slot, m, plru_ref, prest_ref, y_ref, sa_ref, sb_ref,
                                cw, cb, wg_ref[m], ba, bx, lsl4)
            if slot + 1 < LRU_SUB:
                yield
        h_ref[m] = jnp.broadcast_to(hc, (SUBLANES, LANES))
        plru_ref[m, 0:SUBLANES, :] = plru_ref[m, pl.ds(MIX_TS, SUBLANES), :]

    def retention_head(m):
        unit = unit_ref[m]
        gc, gnw = unit[U_CHUNK_DECAY:U_CHUNK_DECAY + 1], unit[U_GN_W:U_GN_W + 1]
        state = state_ref[m]
        for c in range(MIX_TS // RET_C):
            rows = slice(c * RET_C, (c + 1) * RET_C)
            qb = prest_ref[P_QR * N_LRU_LB + m, rows, :]
            qxi = prest_ref[P_QXI * N_LRU_LB + m, rows, :]
            kb = prest_ref[P_KR * N_LRU_LB + m, rows, :]
            kz = prest_ref[P_KZ * N_LRU_LB + m, rows, :]
            vb = prest_ref[P_V * N_LRU_LB + m, rows, :]
            silu_g = prest_ref[P_SILU * N_LRU_LB + m, rows, :].astype(F32)
            scores = lax.dot_general(qb, kb, (((1,), (1,)), ((), ())), preferred_element_type=F32)
            p = (scores * dec_ref[m]).astype(BF16)
            lhs = jnp.concatenate([p, qxi], axis=1)
            rhs = jnp.concatenate([vb, state.astype(BF16)], axis=0)
            o = jnp.dot(lhs, rhs, preferred_element_type=F32)
            kv = lax.dot_general(kz, vb, (((0,), (0,)), ((), ())), preferred_element_type=F32)
            state = state * gc + kv
            mu = jnp.mean(o, axis=-1, keepdims=True)
            d = o - mu
            var = jnp.mean(d * d, axis=-1, keepdims=True)
            on = (d * lax.rsqrt(var + EPS)) * gnw
            y_ref[N_LRU_LB + m, rows, :] = (silu_g * on).astype(BF16)
            if c + 1 < MIX_TS // RET_C:
                yield
        state_ref[m] = state

    def projection():
        w_refs = (wx_ref, wlg_ref, wq_ref, wk_ref, wv_ref, wrg_ref)
        w = jnp.concatenate([w_refs[c][...] for c in DOT_CLASS_ORDER], axis=1)
        res = jnp.dot(u_ref[...], w, preferred_element_type=F32)
        for kk in range(MIX_LB):
            cls, unit = DOT_CLASS_ORDER[kk // UNITS], j * UNITS + kk % UNITS
            blk = res[:, kk * LANES:(kk + 1) * LANES]
            if cls == C_LRU_X:
                plru_ref[unit, pl.ds(SUBLANES, MIX_TS), :] = blk
            elif cls == C_LRU_GATE:
                hg = 0.5 * blk
                gelu = hg * jnp.tanh(blk * (GELU_C + (GELU_C * 0.044715) * (blk * blk))) + hg
                prest_ref[P_GELU * N_LRU_LB + unit, :, :] = gelu.astype(BF16)
            elif cls in (C_Q, C_K):
                cos, sin = (cq_ref, sq_ref) if cls == C_Q else (ck_ref, sk_ref)
                rot = blk * cos[...] + pltpu.roll(blk, RET_HD // 2, 1) * sin[...]
                factor = jnp.concatenate([xz_ref[0 if cls == C_Q else 1, unit]] * (MIX_TS // RET_C), axis=0)
                p_rot, p_scaled = (P_QR, P_QXI) if cls == C_Q else (P_KR, P_KZ)
                prest_ref[p_rot * N_LRU_LB + unit, :, :] = rot.astype(BF16)
                prest_ref[p_scaled * N_LRU_LB + unit, :, :] = (rot * factor).astype(BF16)
            elif cls == C_V:
                prest_ref[P_V * N_LRU_LB + unit, :, :] = blk.astype(BF16)
            else:
                prest_ref[P_SILU * N_LRU_LB + unit, :, :] = _silu(blk).astype(BF16)

    first, last = i == 0, i == N_BLK

    @pl.when(first)
    def _():
        projection()

    @pl.when(jnp.logical_not(jnp.logical_or(first, last)))
    def _():
        head_groups()
        projection()

    @pl.when(last)
    def _():
        head_groups()


def _mix(x, ln, w_slabs, unit_tab, wcat, rot, base, decay, xz, w_gate, w_up, w_down, w_out):
    prev_blk = lambda i, j: (jnp.maximum(i - 1, 0), 0, 0, 0)
    const = lambda a: pl.BlockSpec(a.shape, lambda i, j: (0,) * a.ndim, pipeline_mode=pl.Buffered(1))
    chunk = lambda i, j, n: jnp.minimum(i * MIX_J + j, n - 1)
    nt = LRU_R // SUBLANES
    n_steps = N_BLK * MIX_J
    up_rows = D_MODEL // n_steps
    down_rows = 4 * up_rows
    n_up, n_down, n_ff = D_MODEL // up_rows, D_FF // down_rows, D_FF // FFN_TF
    up_in = pl.BlockSpec((up_rows, D_FF), lambda i, j: (chunk(i, j, n_up), 0))
    up_out = pl.BlockSpec((n_ff, up_rows, 2 * FFN_TF), lambda i, j: (0, chunk(i, j, n_up), 0))
    down_io = pl.BlockSpec((down_rows, D_MODEL), lambda i, j: (chunk(i, j, n_down), 0))
    out_io = pl.BlockSpec((up_rows, D_MODEL), lambda i, j: (chunk(i, j, n_up), 0))
    lru_buf = pltpu.VMEM((N_LRU_LB, SUBLANES + MIX_TS, LANES), F32)
    rest_buf = pltpu.VMEM((N_BF16_KINDS * N_LRU_LB, MIX_TS, LANES), BF16)
    return pl.pallas_call(
        _mix_body,
        grid=(N_BLK + 1, MIX_J),
        in_specs=[
            pl.BlockSpec(memory_space=pl.ANY),
            pl.BlockSpec((1, D_MODEL), lambda i, j: (0, 0)),
            *[pl.BlockSpec((D_MODEL, UNITS * LANES), lambda i, j, c=c: (0, c * MIX_J + j))
              for c in range(D_IN // D_LRU)],
            const(unit_tab), const(wcat), const(rot), const(base), const(decay), const(xz),
            up_in, up_in, down_io, out_io,
        ],
        out_specs=[pl.BlockSpec((None, 2 * N_LRU_LB, MIX_TS, LANES), prev_blk), up_out, down_io, out_io],
        out_shape=[
            jax.ShapeDtypeStruct((N_BLK, 2 * N_LRU_LB, MIX_TS, LANES), BF16),
            jax.ShapeDtypeStruct((n_ff, D_MODEL, 2 * FFN_TF), BF16),
            jax.ShapeDtypeStruct(w_down.shape, BF16),
            jax.ShapeDtypeStruct(w_out.shape, BF16),
        ],
        scratch_shapes=[
            pltpu.VMEM((MIX_TS, D_MODEL), BF16),
            lru_buf, rest_buf,
            pltpu.VMEM((UNITS * LRU_SUB, nt, 2 * SUBLANES, LANES), F32),
            pltpu.VMEM((UNITS * LRU_SUB, nt, 2 * SUBLANES, LANES), F32),
            pltpu.VMEM((N_LRU_LB, SUBLANES, LANES), F32),
            pltpu.VMEM((RET_HEADS, RET_HD, RET_HD), F32),
            pltpu.VMEM((MIX_TS, RET_HD), F32),
            pltpu.VMEM((MIX_TS, RET_HD), F32),
            pltpu.VMEM((MIX_TS, RET_HD), F32),
            pltpu.VMEM((MIX_TS, RET_HD), F32),
            pltpu.VMEM((MIX_TS, D_MODEL), F32),
            pltpu.SemaphoreType.DMA(()),
        ],
        compiler_params=_params("arbitrary", "arbitrary"),
        name="mix",
    )(x, ln, *[w_slabs] * (D_IN // D_LRU), unit_tab, wcat, rot, base, decay, xz, w_gate, w_up, w_down, w_out)


def _outproj_body(x_ref, y_ref, wo_ref, o_ref):
    for rb in range(OUT_TS // MIX_TS):
        rows = slice(rb * MIX_TS, (rb + 1) * MIX_TS)
        y = jnp.concatenate([y_ref[rb, lb] for lb in range(2 * N_LRU_LB)], axis=1)
        o_ref[rows, :] = x_ref[rows, :] + jnp.dot(y, wo_ref[...], preferred_element_type=F32)


def _outproj(x, y, w_out):
    return pl.pallas_call(
        _outproj_body,
        grid=(SEQ // OUT_TS,),
        in_specs=[
            pl.BlockSpec((OUT_TS, D_MODEL), lambda i: (i, 0)),
            pl.BlockSpec((OUT_TS // MIX_TS, 2 * N_LRU_LB, MIX_TS, LANES), lambda i: (i, 0, 0, 0)),
            pl.BlockSpec((D_MODEL, D_MODEL), lambda i: (0, 0)),
        ],
        out_specs=pl.BlockSpec((OUT_TS, D_MODEL), lambda i: (i, 0)),
        out_shape=jax.ShapeDtypeStruct((SEQ, D_MODEL), F32),
        compiler_params=_params("parallel"),
        name="outproj",
    )(x, y, w_out)


def _ffn_body(h_hbm, ln2_ref, wgu_ref, wd_ref, fnw_ref, o_ref, u_ref, hbuf_ref, hsem):
    i = pl.program_id(0)
    f = pl.program_id(1)

    def h_copy(blk):
        return pltpu.make_async_copy(h_hbm.at[pl.ds(blk * FFN_TS, FFN_TS), :], hbuf_ref, hsem)

    @pl.when(jnp.logical_and(i == 0, f == 0))
    def _():
        h_copy(0).start()

    @pl.when(f == 0)
    def _():
        h_copy(i).wait()

        def step(c, carry):
            r0 = pl.multiple_of(c * NORM_ROWS, NORM_ROWS)
            hrows = hbuf_ref[pl.ds(r0, NORM_ROWS), :]
            u_ref[pl.ds(r0, NORM_ROWS), :] = _rmsnorm_rows(hrows, ln2_ref[...]).astype(BF16)
            o_ref[pl.ds(r0, NORM_ROWS), :] = hrows
            return carry

        lax.fori_loop(0, FFN_TS // NORM_ROWS, step, 0)

    @pl.when(jnp.logical_and(f == 1, i + 1 < pl.num_programs(0)))
    def _():
        h_copy(i + 1).start()

    for half in range(FFN_TS // FFN_HALF):
        rows = slice(half * FFN_HALF, (half + 1) * FFN_HALF)
        gu = jnp.dot(u_ref[rows, :], wgu_ref[...], preferred_element_type=F32)
        gate, up = gu[:, :FFN_TF], gu[:, FFN_TF:]
        act = (_silu(gate) * up).astype(BF16)
        o_ref[rows, :] += jnp.dot(act, wd_ref[...], preferred_element_type=F32)

    @pl.when(f == pl.num_programs(1) - 1)
    def _():
        def step(c, carry):
            r0 = pl.multiple_of(c * NORM_ROWS, NORM_ROWS)
            rows = pl.ds(r0, NORM_ROWS)
            o_ref[rows, :] = _rmsnorm_rows(o_ref[rows, :], fnw_ref[...])
            return carry

        lax.fori_loop(0, FFN_TS // NORM_ROWS, step, 0)


def _ffn(h1, ln2, wgu, wd, fnw):
    return pl.pallas_call(
        _ffn_body,
        grid=(SEQ // FFN_TS, D_FF // FFN_TF),
        in_specs=[
            pl.BlockSpec(memory_space=pl.ANY),
            pl.BlockSpec((1, D_MODEL), lambda i, f: (0, 0)),
            pl.BlockSpec((None, D_MODEL, 2 * FFN_TF), lambda i, f: (f, 0, 0)),
            pl.BlockSpec((FFN_TF, D_MODEL), lambda i, f: (f, 0)),
            pl.BlockSpec((1, D_MODEL), lambda i, f: (0, 0)),
        ],
        out_specs=pl.BlockSpec((FFN_TS, D_MODEL), lambda i, f: (i, 0)),
        out_shape=jax.ShapeDtypeStruct((SEQ, D_MODEL), F32),
        scratch_shapes=[
            pltpu.VMEM((FFN_TS, D_MODEL), BF16),
            pltpu.VMEM((FFN_TS, D_MODEL), F32),
            pltpu.SemaphoreType.DMA(()),
        ],
        compiler_params=_params("arbitrary", "arbitrary"),
        name="ffn",
    )(h1, ln2, wgu, wd, fnw)


def _retention_tables():
    H, Dh, C = RET_HEADS, RET_HD, RET_C
    inv_freq = ROPE_BASE ** (-np.arange(0, Dh, 2, dtype=np.float64) / Dh)
    inv2 = np.concatenate([inv_freq, inv_freq])
    off = np.arange(MIX_TS, dtype=np.float64)[:, None] * inv2[None, :]
    base = (np.arange(N_BLK, dtype=np.float64) * MIX_TS)[:, None, None] * inv2[None, None, :]
    log_gamma = np.log1p(-np.exp2(-5.0 - np.arange(H, dtype=np.float64)))
    idx = np.arange(C)
    diff = idx[:, None] - idx[None, :]
    decay = np.where(diff >= 0, np.exp(log_gamma[:, None, None] * np.maximum(diff, 0)[None]), 0.0)
    zeta = np.exp(log_gamma[:, None] * (C - 1 - idx)[None, :])
    xi = np.exp(log_gamma[:, None] * (idx + 1)[None, :])
    gc = np.exp(log_gamma * C)
    per_head = lambda t: np.broadcast_to(t[:, :, None], (H, C, Dh))
    tables = (np.stack([np.cos(off), np.sin(off)]),
              np.concatenate([np.cos(base), np.sin(base)], axis=1),
              decay,
              np.stack([per_head(xi), per_head(zeta)]),
              np.broadcast_to(gc[:, None, None], (H, 1, Dh)))
    return tuple(jnp.asarray(np.ascontiguousarray(t, dtype=np.float32)) for t in tables)


def kernel(x, ln1_w, w_in, conv_w, conv_b, gate_a_w, gate_a_b, gate_x_w, gate_x_b, lru_lambda, ret_gn_w,
           w_out, ln2_w, w_ffn_gate, w_ffn_up, w_ffn_down, final_norm_w):
    x2 = x.reshape(SEQ, D_MODEL)
    row = lambda v: v.reshape(1, -1)
    by_lb = lambda v: v.reshape(-1, 1, LANES)
    wcat = (0.5 * jnp.concatenate([gate_a_w[0], gate_x_w[0]], axis=-1)).astype(BF16)
    cw = conv_w[0].reshape(CONV_W, N_LRU_LB, LANES).transpose(1, 0, 2)
    rot, base, decay, xz, gc = _retention_tables()
    unit_tab = jnp.concatenate(
        [cw, by_lb(conv_b[0]), by_lb(gate_a_b[0]), by_lb(gate_x_b[0]), by_lb(lru_lambda[0]), gc,
         by_lb(ret_gn_w[0]), jnp.zeros((N_LRU_LB, UNIT_ROWS - U_GN_W - 1, LANES), F32)], axis=1)
    y, wgu, wd, wo = _mix(x2, row(ln1_w[0]), w_in[0].astype(BF16), unit_tab, wcat, rot, base, decay, xz,
                          w_ffn_gate[0], w_ffn_up[0], w_ffn_down[0], w_out[0])
    h1 = _outproj(x2, y, wo)
    out = _ffn(h1, row(ln2_w[0]), wgu, wd, row(final_norm_w))
    return out.reshape(1, SEQ, D_MODEL)
```

```python
import jax
import jax.numpy as jnp
import numpy as np
from jax import lax
from jax.experimental import pallas as pl
from jax.experimental.pallas import tpu as pltpu

D_MODEL = 2048
SEQ = 8192
D_LRU = 1024
D_RET = 1024
CONV_W = 4
RET_HEADS = 8
RET_HD = 128
ROPE_BASE = 10000.0
D_FF = 5632
D_IN = 6144
EPS = 1e-6
GELU_C = 0.7978845608028654

SUBLANES = 8
LANES = 128
VMEM_LIMIT_BYTES = 56 * 1024 * 1024

F32 = jnp.float32
BF16 = jnp.bfloat16

NORM_ROWS = 128
MIX_TS = 512
MIX_J = 4
MIX_NW = D_IN // MIX_J
MIX_LB = MIX_NW // LANES
N_BLK = SEQ // MIX_TS
N_LRU_LB = D_LRU // LANES
UNITS = N_LRU_LB // MIX_J
C_LRU_X, C_LRU_GATE, C_Q, C_K, C_V, C_RET_GATE = range(6)
DOT_CLASS_ORDER = (C_Q, C_K, C_LRU_GATE, C_RET_GATE, C_LRU_X, C_V)
P_GELU, P_QR, P_QXI, P_KR, P_KZ, P_V, P_SILU = range(7)
N_BF16_KINDS = 7
LRU_R = 128
LRU_SUB = MIX_TS // LRU_R
RET_C = 256
_LRU_PAIR, _RET_PAIR = tuple(("lru", u) for u in range(UNITS)), tuple(("ret", u) for u in range(UNITS))
HEAD_GROUP_ORDER = _LRU_PAIR + _RET_PAIR + _LRU_PAIR + _RET_PAIR + 2 * _LRU_PAIR
OUT_TS = 1024
FFN_TS, FFN_TF = 1024, 512
FFN_HALF = 512
U_CONV_B, U_GATE_A_B, U_GATE_X_B, U_LAMBDA, U_CHUNK_DECAY, U_GN_W = 4, 5, 6, 7, 8, 9
UNIT_ROWS = 16


def _rmsnorm_rows(x, w):
    ms = jnp.mean(x * x, axis=-1, keepdims=True)
    return (x * lax.rsqrt(ms + EPS)) * w


def _silu(x):
    h = 0.5 * x
    return h * jnp.tanh(h) + h


def _params(*sem):
    return pltpu.CompilerParams(dimension_semantics=sem, vmem_limit_bytes=VMEM_LIMIT_BYTES)


def _lru_sub_block(r0, hc, slot, m, rd_xc, rd_rest, y_ref, sa_ref, sb_ref, wcat, ba, bx, lsl4):
    nt = LRU_R // SUBLANES
    xc = rd_xc[m, pl.ds(r0, LRU_R), :]
    gates = jnp.dot(xc.astype(BF16), wcat, preferred_element_type=F32)
    tr = jnp.tanh(gates[:, :LANES] + ba)
    ti = jnp.tanh(gates[:, LANES:] + bx)
    log_a = (tr + 1.0) * lsl4
    a = jnp.exp(log_a)
    v = -jnp.tanh(log_a)
    coef = jnp.where(v > 0.0, v * lax.rsqrt((v + v) * (1.0 + v)), 0.0)
    b = coef * ((ti + 1.0) * xc)
    a3 = a.reshape(nt, SUBLANES, LANES)
    b3 = b.reshape(nt, SUBLANES, LANES)
    for s in (1, 2, 4):
        sa_ref[slot, :, SUBLANES:2 * SUBLANES, :] = a3
        sb_ref[slot, :, SUBLANES:2 * SUBLANES, :] = b3
        a_sh = sa_ref[slot, :, SUBLANES - s:2 * SUBLANES - s, :]
        b_sh = sb_ref[slot, :, SUBLANES - s:2 * SUBLANES - s, :]
        b3 = a3 * b_sh + b3
        a3 = a3 * a_sh
    tiles = []
    for t in range(nt):
        ht = a3[t] * hc + b3[t]
        hc = ht[SUBLANES - 1:SUBLANES, :]
        tiles.append(ht)
    h = jnp.concatenate(tiles, axis=0)
    gelu_g = rd_rest[m, pl.ds(r0, LRU_R), :].astype(F32)
    y_ref[m, pl.ds(r0, LRU_R), :] = (h * gelu_g).astype(BF16)
    return hc


def _mix_body(x_hbm, ln_ref, wx_ref, wlg_ref, wq_ref, wk_ref, wv_ref, wrg_ref,
              unit_ref, wg_ref, rot_ref, base_ref, dec_ref, xz_ref,
              fg_ref, fu_ref, fd_ref, fo_ref,
              y_ref, bgu_ref, bd_ref, bo_ref,
              u_ref, pxc_ref, prest_ref, stage_ref, hist_ref, sa_ref, sb_ref, h_ref, state_ref,
              cq_ref, sq_ref, ck_ref, sk_ref, xbuf_ref, xsem):
    i = pl.program_id(0)
    j = pl.program_id(1)
    nt = LRU_R // SUBLANES

    for f in range(D_FF // FFN_TF):
        bgu_ref[f, :, 0:FFN_TF] = fg_ref[:, f * FFN_TF:(f + 1) * FFN_TF].astype(BF16)
        bgu_ref[f, :, FFN_TF:2 * FFN_TF] = fu_ref[:, f * FFN_TF:(f + 1) * FFN_TF].astype(BF16)
    bd_ref[...] = fd_ref[...].astype(BF16)
    bo_ref[...] = fo_ref[...].astype(BF16)

    @pl.when(jnp.logical_and(i == 0, j == 0))
    def _():
        h_ref[...] = jnp.zeros_like(h_ref)
        state_ref[...] = jnp.zeros_like(state_ref)
        hist_ref[...] = jnp.zeros_like(hist_ref)

    def x_copy(blk):
        return pltpu.make_async_copy(x_hbm.at[pl.ds(blk * MIX_TS, MIX_TS), :], xbuf_ref, xsem)

    @pl.when(jnp.logical_and(i == 0, j == 0))
    def _():
        x_copy(0).start()

    @pl.when(jnp.logical_and(j == 0, i < N_BLK))
    def _():
        x_copy(i).wait()

        def step(r, carry):
            r0 = pl.multiple_of(r * NORM_ROWS, NORM_ROWS)
            x = xbuf_ref[pl.ds(r0, NORM_ROWS), :]
            u_ref[pl.ds(r0, NORM_ROWS), :] = _rmsnorm_rows(x, ln_ref[...]).astype(BF16)
            return carry

        lax.fori_loop(0, MIX_TS // NORM_ROWS, step, 0)

    @pl.when(jnp.logical_and(j == 1, i + 1 < N_BLK))
    def _():
        x_copy(i + 1).start()

    @pl.when(jnp.logical_and(j == 0, i < N_BLK))
    def _():
        cb_, sb_ = base_ref[i][0:1], base_ref[i][1:2]
        cos = cb_ * rot_ref[0] - sb_ * rot_ref[1]
        sin = sb_ * rot_ref[0] + cb_ * rot_ref[1]
        lane = lax.broadcasted_iota(jnp.int32, (1, RET_HD), 1)
        sin = jnp.where(lane < RET_HD // 2, -sin, sin)
        cq_ref[...] = cos
        sq_ref[...] = sin
        ck_ref[...] = cos * (RET_HD ** -0.5)
        sk_ref[...] = sin * (RET_HD ** -0.5)

    def head_groups():
        sa_ref[:, :, 0:SUBLANES, :] = jnp.ones((UNITS * LRU_SUB, nt, SUBLANES, LANES), F32)
        sb_ref[:, :, 0:SUBLANES, :] = jnp.zeros((UNITS * LRU_SUB, nt, SUBLANES, LANES), F32)
        parts = {}
        for uu in range(UNITS):
            parts["lru", uu] = lru_lane_block(j * UNITS + uu, uu * LRU_SUB)
            parts["ret", uu] = retention_head(j * UNITS + uu)
        for key in HEAD_GROUP_ORDER:
            next(parts[key], None)
        for rest in parts.values():
            for _ in rest:
                pass

    def lru_lane_block(m, slot0):
        unit = unit_ref[m]
        ba, bx = 0.5 * unit[U_GATE_A_B:U_GATE_A_B + 1], 0.5 * unit[U_GATE_X_B:U_GATE_X_B + 1]
        lam = unit[U_LAMBDA:U_LAMBDA + 1]
        lsl4 = -4.0 * (jnp.maximum(-lam, 0.0) + jnp.log1p(jnp.exp(-jnp.abs(lam))))
        hc = h_ref[m][0:1, :]
        for slot in range(LRU_SUB):
            hc = _lru_sub_block(slot * LRU_R, hc, slot0 + slot, m, pxc_ref, prest_ref, y_ref, sa_ref, sb_ref,
                                wg_ref[m], ba, bx, lsl4)
            if slot + 1 < LRU_SUB:
                yield
        h_ref[m] = jnp.broadcast_to(hc, (SUBLANES, LANES))

    def retention_head(m):
        unit = unit_ref[m]
        gc, gnw = unit[U_CHUNK_DECAY:U_CHUNK_DECAY + 1], unit[U_GN_W:U_GN_W + 1]
        state = state_ref[m]
        for c in range(MIX_TS // RET_C):
            rows = slice(c * RET_C, (c + 1) * RET_C)
            qb = prest_ref[P_QR * N_LRU_LB + m, rows, :]
            qxi = prest_ref[P_QXI * N_LRU_LB + m, rows, :]
            kb = prest_ref[P_KR * N_LRU_LB + m, rows, :]
            kz = prest_ref[P_KZ * N_LRU_LB + m, rows, :]
            vb = prest_ref[P_V * N_LRU_LB + m, rows, :]
            silu_g = prest_ref[P_SILU * N_LRU_LB + m, rows, :].astype(F32)
            scores = lax.dot_general(qb, kb, (((1,), (1,)), ((), ())), preferred_element_type=F32)
            p = (scores * dec_ref[m]).astype(BF16)
            lhs = jnp.concatenate([p, qxi], axis=1)
            rhs = jnp.concatenate([vb, state.astype(BF16)], axis=0)
            o = jnp.dot(lhs, rhs, preferred_element_type=F32)
            kv = lax.dot_general(kz, vb, (((0,), (0,)), ((), ())), preferred_element_type=F32)
            state = state * gc + kv
            mu = jnp.mean(o, axis=-1, keepdims=True)
            d = o - mu
            var = jnp.mean(d * d, axis=-1, keepdims=True)
            on = (d * lax.rsqrt(var + EPS)) * gnw
            y_ref[N_LRU_LB + m, rows, :] = (silu_g * on).astype(BF16)
            if c + 1 < MIX_TS // RET_C:
                yield
        state_ref[m] = state

    def projection():
        w_refs = (wx_ref, wlg_ref, wq_ref, wk_ref, wv_ref, wrg_ref)
        w = jnp.concatenate([w_refs[c][...] for c in DOT_CLASS_ORDER], axis=1)
        res = jnp.dot(u_ref[...], w, preferred_element_type=F32)
        for kk in range(MIX_LB):
            cls, unit = DOT_CLASS_ORDER[kk // UNITS], j * UNITS + kk % UNITS
            blk = res[:, kk * LANES:(kk + 1) * LANES]
            if cls == C_LRU_X:
                uu = kk % UNITS
                stage_ref[uu, 0:SUBLANES, :] = hist_ref[unit]
                stage_ref[uu, SUBLANES:SUBLANES + MIX_TS, :] = blk
                hist_ref[unit] = blk[MIX_TS - SUBLANES:, :]
                vec = unit_ref[unit]
                xc = vec[U_CONV_B:U_CONV_B + 1]
                for tap in range(CONV_W - 1):
                    first_row = SUBLANES - (CONV_W - 1) + tap
                    xc = xc + stage_ref[uu, first_row:first_row + MIX_TS, :] * vec[tap:tap + 1]
                pxc_ref[unit] = xc + blk * vec[CONV_W - 1:CONV_W]
            elif cls == C_LRU_GATE:
                hg = 0.5 * blk
                gelu = hg * jnp.tanh(blk * (GELU_C + (GELU_C * 0.044715) * (blk * blk))) + hg
                prest_ref[P_GELU * N_LRU_LB + unit, :, :] = gelu.astype(BF16)
            elif cls in (C_Q, C_K):
                cos, sin = (cq_ref, sq_ref) if cls == C_Q else (ck_ref, sk_ref)
                rot = blk * cos[...] + pltpu.roll(blk, RET_HD // 2, 1) * sin[...]
                factor = jnp.concatenate([xz_ref[0 if cls == C_Q else 1, unit]] * (MIX_TS // RET_C), axis=0)
                p_rot, p_scaled = (P_QR, P_QXI) if cls == C_Q else (P_KR, P_KZ)
                prest_ref[p_rot * N_LRU_LB + unit, :, :] = rot.astype(BF16)
                prest_ref[p_scaled * N_LRU_LB + unit, :, :] = (rot * factor).astype(BF16)
            elif cls == C_V:
                prest_ref[P_V * N_LRU_LB + unit, :, :] = blk.astype(BF16)
            else:
                prest_ref[P_SILU * N_LRU_LB + unit, :, :] = _silu(blk).astype(BF16)

    first, last = i == 0, i == N_BLK

    @pl.when(first)
    def _():
        projection()

    @pl.when(jnp.logical_not(jnp.logical_or(first, last)))
    def _():
        head_groups()
        projection()

    @pl.when(last)
    def _():
        head_groups()


def _mix(x, ln, w_slabs, unit_tab, wcat, rot, base, decay, xz, w_gate, w_up, w_down, w_out):
    prev_blk = lambda i, j: (jnp.maximum(i - 1, 0), 0, 0, 0)
    const = lambda a: pl.BlockSpec(a.shape, lambda i, j: (0,) * a.ndim, pipeline_mode=pl.Buffered(1))
    chunk = lambda i, j, n: jnp.minimum(i * MIX_J + j, n - 1)
    nt = LRU_R // SUBLANES
    n_steps = N_BLK * MIX_J
    up_rows = D_MODEL // n_steps
    down_rows = 4 * up_rows
    n_up, n_down, n_ff = D_MODEL // up_rows, D_FF // down_rows, D_FF // FFN_TF
    up_in = pl.BlockSpec((up_rows, D_FF), lambda i, j: (chunk(i, j, n_up), 0))
    up_out = pl.BlockSpec((n_ff, up_rows, 2 * FFN_TF), lambda i, j: (0, chunk(i, j, n_up), 0))
    down_io = pl.BlockSpec((down_rows, D_MODEL), lambda i, j: (chunk(i, j, n_down), 0))
    out_io = pl.BlockSpec((up_rows, D_MODEL), lambda i, j: (chunk(i, j, n_up), 0))
    xc_buf = pltpu.VMEM((N_LRU_LB, MIX_TS, LANES), F32)
    rest_buf = pltpu.VMEM((N_BF16_KINDS * N_LRU_LB, MIX_TS, LANES), BF16)
    return pl.pallas_call(
        _mix_body,
        grid=(N_BLK + 1, MIX_J),
        in_specs=[
            pl.BlockSpec(memory_space=pl.ANY),
            pl.BlockSpec((1, D_MODEL), lambda i, j: (0, 0)),
            *[pl.BlockSpec((D_MODEL, UNITS * LANES), lambda i, j, c=c: (0, c * MIX_J + j))
              for c in range(D_IN // D_LRU)],
            const(unit_tab), const(wcat), const(rot), const(base), const(decay), const(xz),
            up_in, up_in, down_io, out_io,
        ],
        out_specs=[pl.BlockSpec((None, 2 * N_LRU_LB, MIX_TS, LANES), prev_blk), up_out, down_io, out_io],
        out_shape=[
            jax.ShapeDtypeStruct((N_BLK, 2 * N_LRU_LB, MIX_TS, LANES), BF16),
            jax.ShapeDtypeStruct((n_ff, D_MODEL, 2 * FFN_TF), BF16),
            jax.ShapeDtypeStruct(w_down.shape, BF16),
            jax.ShapeDtypeStruct(w_out.shape, BF16),
        ],
        scratch_shapes=[
            pltpu.VMEM((MIX_TS, D_MODEL), BF16),
            xc_buf, rest_buf,
            pltpu.VMEM((UNITS, SUBLANES + MIX_TS, LANES), F32),
            pltpu.VMEM((N_LRU_LB, SUBLANES, LANES), F32),
            pltpu.VMEM((UNITS * LRU_SUB, nt, 2 * SUBLANES, LANES), F32),
            pltpu.VMEM((UNITS * LRU_SUB, nt, 2 * SUBLANES, LANES), F32),
            pltpu.VMEM((N_LRU_LB, SUBLANES, LANES), F32),
            pltpu.VMEM((RET_HEADS, RET_HD, RET_HD), F32),
            pltpu.VMEM((MIX_TS, RET_HD), F32),
            pltpu.VMEM((MIX_TS, RET_HD), F32),
            pltpu.VMEM((MIX_TS, RET_HD), F32),
            pltpu.VMEM((MIX_TS, RET_HD), F32),
            pltpu.VMEM((MIX_TS, D_MODEL), F32),
            pltpu.SemaphoreType.DMA(()),
        ],
        compiler_params=_params("arbitrary", "arbitrary"),
        name="mix",
    )(x, ln, *[w_slabs] * (D_IN // D_LRU), unit_tab, wcat, rot, base, decay, xz, w_gate, w_up, w_down, w_out)


def _outproj_body(x_ref, y_ref, wo_ref, o_ref):
    for rb in range(OUT_TS // MIX_TS):
        rows = slice(rb * MIX_TS, (rb + 1) * MIX_TS)
        y = jnp.concatenate([y_ref[rb, lb] for lb in range(2 * N_LRU_LB)], axis=1)
        o_ref[rows, :] = x_ref[rows, :] + jnp.dot(y, wo_ref[...], preferred_element_type=F32)


def _outproj(x, y, w_out):
    return pl.pallas_call(
        _outproj_body,
        grid=(SEQ // OUT_TS,),
        in_specs=[
            pl.BlockSpec((OUT_TS, D_MODEL), lambda i: (i, 0)),
            pl.BlockSpec((OUT_TS // MIX_TS, 2 * N_LRU_LB, MIX_TS, LANES), lambda i: (i, 0, 0, 0)),
            pl.BlockSpec((D_MODEL, D_MODEL), lambda i: (0, 0)),
        ],
        out_specs=pl.BlockSpec((OUT_TS, D_MODEL), lambda i: (i, 0)),
        out_shape=jax.ShapeDtypeStruct((SEQ, D_MODEL), F32),
        compiler_params=_params("parallel"),
        name="outproj",
    )(x, y, w_out)


def _ffn_body(h_hbm, ln2_ref, wgu_ref, wd_ref, fnw_ref, o_ref, u_ref, hbuf_ref, hsem):
    i = pl.program_id(0)
    f = pl.program_id(1)

    def h_copy(blk):
        return pltpu.make_async_copy(h_hbm.at[pl.ds(blk * FFN_TS, FFN_TS), :], hbuf_ref, hsem)

    @pl.when(jnp.logical_and(i == 0, f == 0))
    def _():
        h_copy(0).start()

    @pl.when(f == 0)
    def _():
        h_copy(i).wait()

        def step(c, carry):
            r0 = pl.multiple_of(c * NORM_ROWS, NORM_ROWS)
            hrows = hbuf_ref[pl.ds(r0, NORM_ROWS), :]
            u_ref[pl.ds(r0, NORM_ROWS), :] = _rmsnorm_rows(hrows, ln2_ref[...]).astype(BF16)
            o_ref[pl.ds(r0, NORM_ROWS), :] = hrows
            return carry

        lax.fori_loop(0, FFN_TS // NORM_ROWS, step, 0)

    @pl.when(jnp.logical_and(f == 1, i + 1 < pl.num_programs(0)))
    def _():
        h_copy(i + 1).start()

    for half in range(FFN_TS // FFN_HALF):
        rows = slice(half * FFN_HALF, (half + 1) * FFN_HALF)
        gu = jnp.dot(u_ref[rows, :], wgu_ref[...], preferred_element_type=F32)
        gate, up = gu[:, :FFN_TF], gu[:, FFN_TF:]
        act = (_silu(gate) * up).astype(BF16)
        o_ref[rows, :] += jnp.dot(act, wd_ref[...], preferred_element_type=F32)

    @pl.when(f == pl.num_programs(1) - 1)
    def _():
        def step(c, carry):
            r0 = pl.multiple_of(c * NORM_ROWS, NORM_ROWS)
            rows = pl.ds(r0, NORM_ROWS)
            o_ref[rows, :] = _rmsnorm_rows(o_ref[rows, :], fnw_ref[...])
            return carry

        lax.fori_loop(0, FFN_TS // NORM_ROWS, step, 0)


def _ffn(h1, ln2, wgu, wd, fnw):
    return pl.pallas_call(
        _ffn_body,
        grid=(SEQ // FFN_TS, D_FF // FFN_TF),
        in_specs=[
            pl.BlockSpec(memory_space=pl.ANY),
            pl.BlockSpec((1, D_MODEL), lambda i, f: (0, 0)),
            pl.BlockSpec((None, D_MODEL, 2 * FFN_TF), lambda i, f: (f, 0, 0)),
            pl.BlockSpec((FFN_TF, D_MODEL), lambda i, f: (f, 0)),
            pl.BlockSpec((1, D_MODEL), lambda i, f: (0, 0)),
        ],
        out_specs=pl.BlockSpec((FFN_TS, D_MODEL), lambda i, f: (i, 0)),
        out_shape=jax.ShapeDtypeStruct((SEQ, D_MODEL), F32),
        scratch_shapes=[
            pltpu.VMEM((FFN_TS, D_MODEL), BF16),
            pltpu.VMEM((FFN_TS, D_MODEL), F32),
            pltpu.SemaphoreType.DMA(()),
        ],
        compiler_params=_params("arbitrary", "arbitrary"),
        name="ffn",
    )(h1, ln2, wgu, wd, fnw)


def _retention_tables():
    H, Dh, C = RET_HEADS, RET_HD, RET_C
    inv_freq = ROPE_BASE ** (-np.arange(0, Dh, 2, dtype=np.float64) / Dh)
    inv2 = np.concatenate([inv_freq, inv_freq])
    off = np.arange(MIX_TS, dtype=np.float64)[:, None] * inv2[None, :]
    base = (np.arange(N_BLK, dtype=np.float64) * MIX_TS)[:, None, None] * inv2[None, None, :]
    log_gamma = np.log1p(-np.exp2(-5.0 - np.arange(H, dtype=np.float64)))
    idx = np.arange(C)
    diff = idx[:, None] - idx[None, :]
    decay = np.where(diff >= 0, np.exp(log_gamma[:, None, None] * np.maximum(diff, 0)[None]), 0.0)
    zeta = np.exp(log_gamma[:, None] * (C - 1 - idx)[None, :])
    xi = np.exp(log_gamma[:, None] * (idx + 1)[None, :])
    gc = np.exp(log_gamma * C)
    per_head = lambda t: np.broadcast_to(t[:, :, None], (H, C, Dh))
    tables = (np.stack([np.cos(off), np.sin(off)]),
              np.concatenate([np.cos(base), np.sin(base)], axis=1),
              decay,
              np.stack([per_head(xi), per_head(zeta)]),
              np.broadcast_to(gc[:, None, None], (H, 1, Dh)))
    return tuple(jnp.asarray(np.ascontiguousarray(t, dtype=np.float32)) for t in tables)


def kernel(x, ln1_w, w_in, conv_w, conv_b, gate_a_w, gate_a_b, gate_x_w, gate_x_b, lru_lambda, ret_gn_w,
           w_out, ln2_w, w_ffn_gate, w_ffn_up, w_ffn_down, final_norm_w):
    x2 = x.reshape(SEQ, D_MODEL)
    row = lambda v: v.reshape(1, -1)
    by_lb = lambda v: v.reshape(-1, 1, LANES)
    wcat = (0.5 * jnp.concatenate([gate_a_w[0], gate_x_w[0]], axis=-1)).astype(BF16)
    cw = conv_w[0].reshape(CONV_W, N_LRU_LB, LANES).transpose(1, 0, 2)
    rot, base, decay, xz, gc = _retention_tables()
    unit_tab = jnp.concatenate(
        [cw, by_lb(conv_b[0]), by_lb(gate_a_b[0]), by_lb(gate_x_b[0]), by_lb(lru_lambda[0]), gc,
         by_lb(ret_gn_w[0]), jnp.zeros((N_LRU_LB, UNIT_ROWS - U_GN_W - 1, LANES), F32)], axis=1)
    y, wgu, wd, wo = _mix(x2, row(ln1_w[0]), w_in[0].astype(BF16), unit_tab, wcat, rot, base, decay, xz,
                          w_ffn_gate[0], w_ffn_up[0], w_ffn_down[0], w_out[0])
    h1 = _outproj(x2, y, wo)
    out = _ffn(h1, row(ln2_w[0]), wgu, wd, row(final_norm_w))
    return out.reshape(1, SEQ, D_MODEL)
```

```python
import jax
import jax.numpy as jnp
import numpy as np
from jax import lax
from jax.experimental import pallas as pl
from jax.experimental.pallas import tpu as pltpu

D_MODEL = 2048
SEQ = 8192
D_LRU = 1024
D_RET = 1024
CONV_W = 4
RET_HEADS = 8
RET_HD = 128
ROPE_BASE = 10000.0
D_FF = 5632
D_IN = 6144
EPS = 1e-6
GELU_C = 0.7978845608028654

SUBLANES = 8
LANES = 128
VMEM_LIMIT_BYTES = 56 * 1024 * 1024
MANUAL_DMA_PRIORITY = 1

F32 = jnp.float32
BF16 = jnp.bfloat16

NORM_ROWS = 128
MIX_TS = 512
MIX_J = 4
MIX_NW = D_IN // MIX_J
MIX_LB = MIX_NW // LANES
N_BLK = SEQ // MIX_TS
N_LRU_LB = D_LRU // LANES
UNITS = N_LRU_LB // MIX_J
C_LRU_X, C_LRU_GATE, C_Q, C_K, C_V, C_RET_GATE = range(6)
DOT_CLASS_ORDER = (C_Q, C_K, C_LRU_GATE, C_RET_GATE, C_LRU_X, C_V)
P_GELU, P_QR, P_QXI, P_KR, P_KZ, P_V, P_SILU = range(7)
N_BF16_KINDS = 7
LRU_R = 128
LRU_SUB = MIX_TS // LRU_R
RET_C = 256
_LRU_PAIR, _RET_PAIR = tuple(("lru", u) for u in range(UNITS)), tuple(("ret", u) for u in range(UNITS))
HEAD_GROUP_ORDER = _LRU_PAIR + _RET_PAIR + _LRU_PAIR + _RET_PAIR + 2 * _LRU_PAIR
OUT_TS = 1024
FFN_TS, FFN_TF = 1024, 512
FFN_HALF = 512
U_CONV_B, U_GATE_A_B, U_GATE_X_B, U_LAMBDA, U_CHUNK_DECAY, U_GN_W = 4, 5, 6, 7, 8, 9
UNIT_ROWS = 16


def _rmsnorm_rows(x, w):
    ms = jnp.mean(x * x, axis=-1, keepdims=True)
    return (x * lax.rsqrt(ms + EPS)) * w


def _silu(x):
    h = 0.5 * x
    return h * jnp.tanh(h) + h


def _params(*sem):
    return pltpu.CompilerParams(dimension_semantics=sem, vmem_limit_bytes=VMEM_LIMIT_BYTES)


def _lru_sub_block(r0, hc, slot, m, rd_xc, rd_rest, y_ref, sa_ref, sb_ref, wcat, ba, bx, lsl4):
    nt = LRU_R // SUBLANES
    xc = rd_xc[m, pl.ds(r0, LRU_R), :]
    gates = jnp.dot(xc.astype(BF16), wcat, preferred_element_type=F32)
    tr = jnp.tanh(gates[:, :LANES] + ba)
    ti = jnp.tanh(gates[:, LANES:] + bx)
    log_a = (tr + 1.0) * lsl4
    a = jnp.exp(log_a)
    v = -jnp.tanh(log_a)
    coef = jnp.where(v > 0.0, v * lax.rsqrt((v + v) * (1.0 + v)), 0.0)
    b = coef * ((ti + 1.0) * xc)
    a3 = a.reshape(nt, SUBLANES, LANES)
    b3 = b.reshape(nt, SUBLANES, LANES)
    for s in (1, 2, 4):
        sa_ref[slot, :, SUBLANES:2 * SUBLANES, :] = a3
        sb_ref[slot, :, SUBLANES:2 * SUBLANES, :] = b3
        a_sh = sa_ref[slot, :, SUBLANES - s:2 * SUBLANES - s, :]
        b_sh = sb_ref[slot, :, SUBLANES - s:2 * SUBLANES - s, :]
        b3 = a3 * b_sh + b3
        a3 = a3 * a_sh
    tiles = []
    for t in range(nt):
        ht = a3[t] * hc + b3[t]
        hc = ht[SUBLANES - 1:SUBLANES, :]
        tiles.append(ht)
    h = jnp.concatenate(tiles, axis=0)
    gelu_g = rd_rest[m, pl.ds(r0, LRU_R), :].astype(F32)
    y_ref[m, pl.ds(r0, LRU_R), :] = (h * gelu_g).astype(BF16)
    return hc


def _mix_body(x_hbm, ln_ref, wx_ref, wlg_ref, wq_ref, wk_ref, wv_ref, wrg_ref,
              unit_ref, wg_ref, rot_ref, base_ref, dec_ref, xz_ref,
              fg_ref, fu_ref, fd_ref, fo_ref,
              y_ref, bgu_ref, bd_ref, bo_ref,
              u_ref, pxc_ref, prest_ref, stage_ref, hist_ref, sa_ref, sb_ref, h_ref, state_ref,
              cq_ref, sq_ref, ck_ref, sk_ref, xbuf_ref, xsem):
    i = pl.program_id(0)
    j = pl.program_id(1)
    nt = LRU_R // SUBLANES

    for f in range(D_FF // FFN_TF):
        bgu_ref[f, :, 0:FFN_TF] = fg_ref[:, f * FFN_TF:(f + 1) * FFN_TF].astype(BF16)
        bgu_ref[f, :, FFN_TF:2 * FFN_TF] = fu_ref[:, f * FFN_TF:(f + 1) * FFN_TF].astype(BF16)
    bd_ref[...] = fd_ref[...].astype(BF16)
    bo_ref[...] = fo_ref[...].astype(BF16)

    @pl.when(jnp.logical_and(i == 0, j == 0))
    def _():
        h_ref[...] = jnp.zeros_like(h_ref)
        state_ref[...] = jnp.zeros_like(state_ref)
        hist_ref[...] = jnp.zeros_like(hist_ref)

    def x_copy(blk):
        return pltpu.make_async_copy(x_hbm.at[pl.ds(blk * MIX_TS, MIX_TS), :], xbuf_ref, xsem)

    @pl.when(jnp.logical_and(i == 0, j == 0))
    def _():
        x_copy(0).start(priority=MANUAL_DMA_PRIORITY)

    @pl.when(jnp.logical_and(j == 0, i < N_BLK))
    def _():
        x_copy(i).wait()

        def step(r, carry):
            r0 = pl.multiple_of(r * NORM_ROWS, NORM_ROWS)
            x = xbuf_ref[pl.ds(r0, NORM_ROWS), :]
            u_ref[pl.ds(r0, NORM_ROWS), :] = _rmsnorm_rows(x, ln_ref[...]).astype(BF16)
            return carry

        lax.fori_loop(0, MIX_TS // NORM_ROWS, step, 0)

    @pl.when(jnp.logical_and(j == 1, i + 1 < N_BLK))
    def _():
        x_copy(i + 1).start(priority=MANUAL_DMA_PRIORITY)

    @pl.when(jnp.logical_and(j == 0, i < N_BLK))
    def _():
        cb_, sb_ = base_ref[i][0:1], base_ref[i][1:2]
        cos = cb_ * rot_ref[0] - sb_ * rot_ref[1]
        sin = sb_ * rot_ref[0] + cb_ * rot_ref[1]
        lane = lax.broadcasted_iota(jnp.int32, (1, RET_HD), 1)
        sin = jnp.where(lane < RET_HD // 2, -sin, sin)
        cq_ref[...] = cos
        sq_ref[...] = sin
        ck_ref[...] = cos * (RET_HD ** -0.5)
        sk_ref[...] = sin * (RET_HD ** -0.5)

    def head_groups():
        sa_ref[:, :, 0:SUBLANES, :] = jnp.ones((UNITS * LRU_SUB, nt, SUBLANES, LANES), F32)
        sb_ref[:, :, 0:SUBLANES, :] = jnp.zeros((UNITS * LRU_SUB, nt, SUBLANES, LANES), F32)
        parts = {}
        for uu in range(UNITS):
            parts["lru", uu] = lru_lane_block(j * UNITS + uu, uu * LRU_SUB)
            parts["ret", uu] = retention_head(j * UNITS + uu)
        for key in HEAD_GROUP_ORDER:
            next(parts[key], None)
        for rest in parts.values():
            for _ in rest:
                pass

    def lru_lane_block(m, slot0):
        unit = unit_ref[m]
        ba, bx = 0.5 * unit[U_GATE_A_B:U_GATE_A_B + 1], 0.5 * unit[U_GATE_X_B:U_GATE_X_B + 1]
        lam = unit[U_LAMBDA:U_LAMBDA + 1]
        lsl4 = -4.0 * (jnp.maximum(-lam, 0.0) + jnp.log1p(jnp.exp(-jnp.abs(lam))))
        hc = h_ref[m][0:1, :]
        for slot in range(LRU_SUB):
            hc = _lru_sub_block(slot * LRU_R, hc, slot0 + slot, m, pxc_ref, prest_ref, y_ref, sa_ref, sb_ref,
                                wg_ref[m], ba, bx, lsl4)
            if slot + 1 < LRU_SUB:
                yield
        h_ref[m] = jnp.broadcast_to(hc, (SUBLANES, LANES))

    def retention_head(m):
        unit = unit_ref[m]
        gc, gnw = unit[U_CHUNK_DECAY:U_CHUNK_DECAY + 1], unit[U_GN_W:U_GN_W + 1]
        state = state_ref[m]
        for c in range(MIX_TS // RET_C):
            rows = slice(c * RET_C, (c + 1) * RET_C)
            qb = prest_ref[P_QR * N_LRU_LB + m, rows, :]
            qxi = prest_ref[P_QXI * N_LRU_LB + m, rows, :]
            kb = prest_ref[P_KR * N_LRU_LB + m, rows, :]
            kz = prest_ref[P_KZ * N_LRU_LB + m, rows, :]
            vb = prest_ref[P_V * N_LRU_LB + m, rows, :]
            silu_g = prest_ref[P_SILU * N_LRU_LB + m, rows, :].astype(F32)
            scores = lax.dot_general(qb, kb, (((1,), (1,)), ((), ())), preferred_element_type=F32)
            p = (scores * dec_ref[m]).astype(BF16)
            lhs = jnp.concatenate([p, qxi], axis=1)
            rhs = jnp.concatenate([vb, state.astype(BF16)], axis=0)
            o = jnp.dot(lhs, rhs, preferred_element_type=F32)
            kv = lax.dot_general(kz, vb, (((0,), (0,)), ((), ())), preferred_element_type=F32)
            state = state * gc + kv
            mu = jnp.mean(o, axis=-1, keepdims=True)
            d = o - mu
            var = jnp.mean(d * d, axis=-1, keepdims=True)
            on = (d * lax.rsqrt(var + EPS)) * gnw
            y_ref[N_LRU_LB + m, rows, :] = (silu_g * on).astype(BF16)
            if c + 1 < MIX_TS // RET_C:
                yield
        state_ref[m] = state

    def projection():
        w_refs = (wx_ref, wlg_ref, wq_ref, wk_ref, wv_ref, wrg_ref)
        w = jnp.concatenate([w_refs[c][...] for c in DOT_CLASS_ORDER], axis=1)
        res = jnp.dot(u_ref[...], w, preferred_element_type=F32)
        for kk in range(MIX_LB):
            cls, unit = DOT_CLASS_ORDER[kk // UNITS], j * UNITS + kk % UNITS
            blk = res[:, kk * LANES:(kk + 1) * LANES]
            if cls == C_LRU_X:
                uu = kk % UNITS
                stage_ref[uu, 0:SUBLANES, :] = hist_ref[unit]
                stage_ref[uu, SUBLANES:SUBLANES + MIX_TS, :] = blk
                hist_ref[unit] = blk[MIX_TS - SUBLANES:, :]
                vec = unit_ref[unit]
                xc = vec[U_CONV_B:U_CONV_B + 1]
                for tap in range(CONV_W - 1):
                    first_row = SUBLANES - (CONV_W - 1) + tap
                    xc = xc + stage_ref[uu, first_row:first_row + MIX_TS, :] * vec[tap:tap + 1]
                pxc_ref[unit] = xc + blk * vec[CONV_W - 1:CONV_W]
            elif cls == C_LRU_GATE:
                hg = 0.5 * blk
                gelu = hg * jnp.tanh(blk * (GELU_C + (GELU_C * 0.044715) * (blk * blk))) + hg
                prest_ref[P_GELU * N_LRU_LB + unit, :, :] = gelu.astype(BF16)
            elif cls in (C_Q, C_K):
                cos, sin = (cq_ref, sq_ref) if cls == C_Q else (ck_ref, sk_ref)
                rot = blk * cos[...] + pltpu.roll(blk, RET_HD // 2, 1) * sin[...]
                factor = jnp.concatenate([xz_ref[0 if cls == C_Q else 1, unit]] * (MIX_TS // RET_C), axis=0)
                p_rot, p_scaled = (P_QR, P_QXI) if cls == C_Q else (P_KR, P_KZ)
                prest_ref[p_rot * N_LRU_LB + unit, :, :] = rot.astype(BF16)
                prest_ref[p_scaled * N_LRU_LB + unit, :, :] = (rot * factor).astype(BF16)
            elif cls == C_V:
                prest_ref[P_V * N_LRU_LB + unit, :, :] = blk.astype(BF16)
            else:
                prest_ref[P_SILU * N_LRU_LB + unit, :, :] = _silu(blk).astype(BF16)

    first, last = i == 0, i == N_BLK

    @pl.when(first)
    def _():
        projection()

    @pl.when(jnp.logical_not(jnp.logical_or(first, last)))
    def _():
        head_groups()
        projection()

    @pl.when(last)
    def _():
        head_groups()


def _mix(x, ln, w_slabs, unit_tab, wcat, rot, base, decay, xz, w_gate, w_up, w_down, w_out):
    prev_blk = lambda i, j: (jnp.maximum(i - 1, 0), 0, 0, 0)
    const = lambda a: pl.BlockSpec(a.shape, lambda i, j: (0,) * a.ndim, pipeline_mode=pl.Buffered(1))
    chunk = lambda i, j, n: jnp.minimum(i * MIX_J + j, n - 1)
    nt = LRU_R // SUBLANES
    n_steps = N_BLK * MIX_J
    up_rows = D_MODEL // n_steps
    down_rows = 4 * up_rows
    n_up, n_down, n_ff = D_MODEL // up_rows, D_FF // down_rows, D_FF // FFN_TF
    up_in = pl.BlockSpec((up_rows, D_FF), lambda i, j: (chunk(i, j, n_up), 0))
    up_out = pl.BlockSpec((n_ff, up_rows, 2 * FFN_TF), lambda i, j: (0, chunk(i, j, n_up), 0))
    down_io = pl.BlockSpec((down_rows, D_MODEL), lambda i, j: (chunk(i, j, n_down), 0))
    out_io = pl.BlockSpec((up_rows, D_MODEL), lambda i, j: (chunk(i, j, n_up), 0))
    xc_buf = pltpu.VMEM((N_LRU_LB, MIX_TS, LANES), F32)
    rest_buf = pltpu.VMEM((N_BF16_KINDS * N_LRU_LB, MIX_TS, LANES), BF16)
    return pl.pallas_call(
        _mix_body,
        grid=(N_BLK + 1, MIX_J),
        in_specs=[
            pl.BlockSpec(memory_space=pl.ANY),
            pl.BlockSpec((1, D_MODEL), lambda i, j: (0, 0)),
            *[pl.BlockSpec((D_MODEL, UNITS * LANES), lambda i, j, c=c: (0, c * MIX_J + j))
              for c in range(D_IN // D_LRU)],
            const(unit_tab), const(wcat), const(rot), const(base), const(decay), const(xz),
            up_in, up_in, down_io, out_io,
        ],
        out_specs=[pl.BlockSpec((None, 2 * N_LRU_LB, MIX_TS, LANES), prev_blk), up_out, down_io, out_io],
        out_shape=[
            jax.ShapeDtypeStruct((N_BLK, 2 * N_LRU_LB, MIX_TS, LANES), BF16),
            jax.ShapeDtypeStruct((n_ff, D_MODEL, 2 * FFN_TF), BF16),
            jax.ShapeDtypeStruct(w_down.shape, BF16),
            jax.ShapeDtypeStruct(w_out.shape, BF16),
        ],
        scratch_shapes=[
            pltpu.VMEM((MIX_TS, D_MODEL), BF16),
            xc_buf, rest_buf,
            pltpu.VMEM((UNITS, SUBLANES + MIX_TS, LANES), F32),
            pltpu.VMEM((N_LRU_LB, SUBLANES, LANES), F32),
            pltpu.VMEM((UNITS * LRU_SUB, nt, 2 * SUBLANES, LANES), F32),
            pltpu.VMEM((UNITS * LRU_SUB, nt, 2 * SUBLANES, LANES), F32),
            pltpu.VMEM((N_LRU_LB, SUBLANES, LANES), F32),
            pltpu.VMEM((RET_HEADS, RET_HD, RET_HD), F32),
            pltpu.VMEM((MIX_TS, RET_HD), F32),
            pltpu.VMEM((MIX_TS, RET_HD), F32),
            pltpu.VMEM((MIX_TS, RET_HD), F32),
            pltpu.VMEM((MIX_TS, RET_HD), F32),
            pltpu.VMEM((MIX_TS, D_MODEL), F32),
            pltpu.SemaphoreType.DMA(()),
        ],
        compiler_params=_params("arbitrary", "arbitrary"),
        name="mix",
    )(x, ln, *[w_slabs] * (D_IN // D_LRU), unit_tab, wcat, rot, base, decay, xz, w_gate, w_up, w_down, w_out)


def _outproj_body(x_ref, y_ref, wo_ref, o_ref):
    for rb in range(OUT_TS // MIX_TS):
        rows = slice(rb * MIX_TS, (rb + 1) * MIX_TS)
        y = jnp.concatenate([y_ref[rb, lb] for lb in range(2 * N_LRU_LB)], axis=1)
        o_ref[rows, :] = x_ref[rows, :] + jnp.dot(y, wo_ref[...], preferred_element_type=F32)


def _outproj(x, y, w_out):
    return pl.pallas_call(
        _outproj_body,
        grid=(SEQ // OUT_TS,),
        in_specs=[
            pl.BlockSpec((OUT_TS, D_MODEL), lambda i: (i, 0)),
            pl.BlockSpec((OUT_TS // MIX_TS, 2 * N_LRU_LB, MIX_TS, LANES), lambda i: (i, 0, 0, 0)),
            pl.BlockSpec((D_MODEL, D_MODEL), lambda i: (0, 0)),
        ],
        out_specs=pl.BlockSpec((OUT_TS, D_MODEL), lambda i: (i, 0)),
        out_shape=jax.ShapeDtypeStruct((SEQ, D_MODEL), F32),
        compiler_params=_params("parallel"),
        name="outproj",
    )(x, y, w_out)


def _ffn_body(h_hbm, ln2_ref, wgu_ref, wd_ref, fnw_ref, o_ref, u_ref, hbuf_ref, hsem):
    i = pl.program_id(0)
    f = pl.program_id(1)

    def h_copy(blk):
        return pltpu.make_async_copy(h_hbm.at[pl.ds(blk * FFN_TS, FFN_TS), :], hbuf_ref, hsem)

    @pl.when(jnp.logical_and(i == 0, f == 0))
    def _():
        h_copy(0).start(priority=MANUAL_DMA_PRIORITY)

    @pl.when(f == 0)
    def _():
        h_copy(i).wait()

        def step(c, carry):
            r0 = pl.multiple_of(c * NORM_ROWS, NORM_ROWS)
            hrows = hbuf_ref[pl.ds(r0, NORM_ROWS), :]
            u_ref[pl.ds(r0, NORM_ROWS), :] = _rmsnorm_rows(hrows, ln2_ref[...]).astype(BF16)
            o_ref[pl.ds(r0, NORM_ROWS), :] = hrows
            return carry

        lax.fori_loop(0, FFN_TS // NORM_ROWS, step, 0)

    @pl.when(jnp.logical_and(f == 1, i + 1 < pl.num_programs(0)))
    def _():
        h_copy(i + 1).start(priority=MANUAL_DMA_PRIORITY)

    for half in range(FFN_TS // FFN_HALF):
        rows = slice(half * FFN_HALF, (half + 1) * FFN_HALF)
        gu = jnp.dot(u_ref[rows, :], wgu_ref[...], preferred_element_type=F32)
        gate, up = gu[:, :FFN_TF], gu[:, FFN_TF:]
        act = (_silu(gate) * up).astype(BF16)
        o_ref[rows, :] += jnp.dot(act, wd_ref[...], preferred_element_type=F32)

    @pl.when(f == pl.num_programs(1) - 1)
    def _():
        def step(c, carry):
            r0 = pl.multiple_of(c * NORM_ROWS, NORM_ROWS)
            rows = pl.ds(r0, NORM_ROWS)
            o_ref[rows, :] = _rmsnorm_rows(o_ref[rows, :], fnw_ref[...])
            return carry

        lax.fori_loop(0, FFN_TS // NORM_ROWS, step, 0)


def _ffn(h1, ln2, wgu, wd, fnw):
    return pl.pallas_call(
        _ffn_body,
        grid=(SEQ // FFN_TS, D_FF // FFN_TF),
        in_specs=[
            pl.BlockSpec(memory_space=pl.ANY),
            pl.BlockSpec((1, D_MODEL), lambda i, f: (0, 0)),
            pl.BlockSpec((None, D_MODEL, 2 * FFN_TF), lambda i, f: (f, 0, 0)),
            pl.BlockSpec((FFN_TF, D_MODEL), lambda i, f: (f, 0)),
            pl.BlockSpec((1, D_MODEL), lambda i, f: (0, 0)),
        ],
        out_specs=pl.BlockSpec((FFN_TS, D_MODEL), lambda i, f: (i, 0)),
        out_shape=jax.ShapeDtypeStruct((SEQ, D_MODEL), F32),
        scratch_shapes=[
            pltpu.VMEM((FFN_TS, D_MODEL), BF16),
            pltpu.VMEM((FFN_TS, D_MODEL), F32),
            pltpu.SemaphoreType.DMA(()),
        ],
        compiler_params=_params("arbitrary", "arbitrary"),
        name="ffn",
    )(h1, ln2, wgu, wd, fnw)


def _retention_tables():
    H, Dh, C = RET_HEADS, RET_HD, RET_C
    inv_freq = ROPE_BASE ** (-np.arange(0, Dh, 2, dtype=np.float64) / Dh)
    inv2 = np.concatenate([inv_freq, inv_freq])
    off = np.arange(MIX_TS, dtype=np.float64)[:, None] * inv2[None, :]
    base = (np.arange(N_BLK, dtype=np.float64) * MIX_TS)[:, None, None] * inv2[None, None, :]
    log_gamma = np.log1p(-np.exp2(-5.0 - np.arange(H, dtype=np.float64)))
    idx = np.arange(C)
    diff = idx[:, None] - idx[None, :]
    decay = np.where(diff >= 0, np.exp(log_gamma[:, None, None] * np.maximum(diff, 0)[None]), 0.0)
    zeta = np.exp(log_gamma[:, None] * (C - 1 - idx)[None, :])
    xi = np.exp(log_gamma[:, None] * (idx + 1)[None, :])
    gc = np.exp(log_gamma * C)
    per_head = lambda t: np.broadcast_to(t[:, :, None], (H, C, Dh))
    tables = (np.stack([np.cos(off), np.sin(off)]),
              np.concatenate([np.cos(base), np.sin(base)], axis=1),
              decay,
              np.stack([per_head(xi), per_head(zeta)]),
              np.broadcast_to(gc[:, None, None], (H, 1, Dh)))
    return tuple(jnp.asarray(np.ascontiguousarray(t, dtype=np.float32)) for t in tables)


def kernel(x, ln1_w, w_in, conv_w, conv_b, gate_a_w, gate_a_b, gate_x_w, gate_x_b, lru_lambda, ret_gn_w,
           w_out, ln2_w, w_ffn_gate, w_ffn_up, w_ffn_down, final_norm_w):
    x2 = x.reshape(SEQ, D_MODEL)
    row = lambda v: v.reshape(1, -1)
    by_lb = lambda v: v.reshape(-1, 1, LANES)
    wcat = (0.5 * jnp.concatenate([gate_a_w[0], gate_x_w[0]], axis=-1)).astype(BF16)
    cw = conv_w[0].reshape(CONV_W, N_LRU_LB, LANES).transpose(1, 0, 2)
    rot, base, decay, xz, gc = _retention_tables()
    unit_tab = jnp.concatenate(
        [cw, by_lb(conv_b[0]), by_lb(gate_a_b[0]), by_lb(gate_x_b[0]), by_lb(lru_lambda[0]), gc,
         by_lb(ret_gn_w[0]), jnp.zeros((N_LRU_LB, UNIT_ROWS - U_GN_W - 1, LANES), F32)], axis=1)
    y, wgu, wd, wo = _mix(x2, row(ln1_w[0]), w_in[0].astype(BF16), unit_tab, wcat, rot, base, decay, xz,
                          w_ffn_gate[0], w_ffn_up[0], w_ffn_down[0], w_out[0])
    h1 = _outproj(x2, y, wo)
    out = _ffn(h1, row(ln2_w[0]), wgu, wd, row(final_norm_w))
    return out.reshape(1, SEQ, D_MODEL)
```
